```python
import jax, jax.numpy as jnp
from jax import lax
import numpy as np

D_MODEL = 1024
BATCH = 8
SEQ = 4096
DEPTH = 1

N_META = 16
GRID_W = 64
D_ATTN = D_MODEL // 2
N_HEADS = 8
N_KV_HEADS = 2
HEAD_DIM = D_ATTN // N_HEADS
KV_GROUP = N_HEADS // N_KV_HEADS
D_KV = N_KV_HEADS * HEAD_DIM
ROPE_AXIS_PAIRS = HEAD_DIM // 4
ROPE_THETA = 10000.0
Q_BLOCK = 128
D_POOL = D_MODEL // 2
POOL_WINDOWS = (2, 4, 8, 16)
N_POOL_GROUPS = len(POOL_WINDOWS)
POOL_GROUP = D_POOL // N_POOL_GROUPS
D_MIX = D_ATTN + D_POOL
D_IN = D_ATTN + 2 * D_KV + D_POOL
D_FF = -(-8 * D_MODEL // (3 * 256)) * 256
EPS = 1e-6

kernel_name = "hybrid_gqa_axialrope_multiscale_pool_swiglu"


def rmsnorm(x, g):
    xf = x.astype(jnp.float32)
    y = xf * lax.rsqrt(jnp.mean(xf * xf, axis=-1, keepdims=True) + EPS)
    return (y * g.astype(jnp.float32)).astype(x.dtype)


def axial_rope_tables(L, n_real):
    rows = n_real // GRID_W
    t = jnp.arange(L, dtype=jnp.int32)
    r = t - N_META
    real = r >= 0
    row = jnp.where(real, r // GRID_W - rows // 2, 0).astype(jnp.float32)
    col = jnp.where(real, r % GRID_W - GRID_W // 2, 0).astype(jnp.float32)
    freqs = ROPE_THETA ** (-jnp.arange(ROPE_AXIS_PAIRS, dtype=jnp.float32) / ROPE_AXIS_PAIRS)
    ang = jnp.stack([row[:, None] * freqs, col[:, None] * freqs], axis=1)
    return jnp.cos(ang), jnp.sin(ang)


def apply_axial_rope(x, cos, sin):
    B, L, H, _ = x.shape
    xf = x.astype(jnp.float32).reshape(B, L, H, 2, 2, ROPE_AXIS_PAIRS)
    x0, x1 = xf[..., 0, :], xf[..., 1, :]
    c = cos[None, :, None]
    s = sin[None, :, None]
    out = jnp.stack([x0 * c - x1 * s, x0 * s + x1 * c], axis=-2)
    return out.reshape(x.shape).astype(x.dtype)


def attend_block(qb, k, v):
    B, Tq = qb.shape[0], qb.shape[1]
    qg = qb.reshape(B, Tq, N_KV_HEADS, KV_GROUP, HEAD_DIM)
    s = jnp.einsum('bqkgd,bskd->bkgqs', qg, k).astype(jnp.float32) * (HEAD_DIM ** -0.5)
    p = jax.nn.softmax(s, axis=-1).astype(v.dtype)
    o = jnp.einsum('bkgqs,bskd->bqkgd', p, v)
    return o.reshape(B, Tq, D_ATTN)


def gqa_bidirectional(q, k, v):
    B, L = q.shape[0], q.shape[1]
    n_real = L - N_META
    nb = n_real // Q_BLOCK
    out_meta = attend_block(q[:, :N_META], k, v)
    q_real = q[:, N_META:].reshape(B, nb, Q_BLOCK, N_HEADS, HEAD_DIM)
    q_real = jnp.moveaxis(q_real, 1, 0)
    out_real = lax.map(lambda qb: attend_block(qb, k, v), q_real)
    out_real = jnp.moveaxis(out_real, 0, 1).reshape(B, n_real, D_ATTN)
    return jnp.concatenate([out_meta, out_real], axis=1)


def multiscale_pool(u, w_pool):
    B, L, _ = u.shape
    t = jnp.arange(L, dtype=jnp.int32)
    diffs = []
    for gi, w in enumerate(POOL_WINDOWS):
        ug = u[..., gi * POOL_GROUP:(gi + 1) * POOL_GROUP].astype(jnp.float32)
        S = jnp.concatenate([jnp.zeros((B, 1, POOL_GROUP), jnp.float32),
                             jnp.cumsum(ug, axis=1)], axis=1)
        lo = jnp.clip(t - w // 2, 0, L)
        hi = jnp.clip(t + w - w // 2, 0, L)
        win_sum = jnp.take(S, hi, axis=1) - jnp.take(S, lo, axis=1)
        cnt = (hi - lo).astype(jnp.float32)[None, :, None]
        diffs.append(win_sum / cnt - ug)
    d = jnp.stack(diffs, axis=2).astype(u.dtype)
    y = jnp.einsum('blgc,gcd->blgd', d, w_pool)
    return y.reshape(B, L, D_POOL)


def setup_inputs(seed: int = 0) -> dict:
    key = jax.random.key(seed)
    ks = jax.random.split(key, 14)
    f32 = jnp.float32

    def gain(k, shape):
        return 1.0 + 0.05 * jax.random.normal(k, shape, f32)

    return {
        "x": jax.random.normal(ks[0], (BATCH, SEQ, D_MODEL), f32),
        "meta_tokens": jax.random.normal(ks[1], (N_META, D_MODEL), f32),
        "norm_mix": gain(ks[2], (DEPTH, D_MODEL)),
        "w_in": jax.random.normal(ks[3], (DEPTH, D_MODEL, D_IN), f32) * D_MODEL ** -0.5,
        "q_norm": gain(ks[4], (DEPTH, HEAD_DIM)),
        "k_norm": gain(ks[5], (DEPTH, HEAD_DIM)),
        "attn_out_norm": gain(ks[6], (DEPTH, D_ATTN)),
        "w_pool": jax.random.normal(ks[7], (DEPTH, N_POOL_GROUPS, POOL_GROUP, POOL_GROUP), f32) * POOL_GROUP ** -0.5,
        "pool_scale": gain(ks[8], (DEPTH, D_POOL)),
        "w_out": jax.random.normal(ks[9], (DEPTH, D_MIX, D_MODEL), f32) * D_MIX ** -0.5,
        "norm_ffn": gain(ks[10], (DEPTH, D_MODEL)),
        "w_ffn_in": jax.random.normal(ks[11], (DEPTH, D_MODEL, 2 * D_FF), f32) * D_MODEL ** -0.5,
        "w_ffn_down": jax.random.normal(ks[12], (DEPTH, D_FF, D_MODEL), f32) * D_FF ** -0.5,
        "norm_final": gain(ks[13], (D_MODEL,)),
    }


def reference(x, meta_tokens, norm_mix, w_in, q_norm, k_norm, attn_out_norm,
              w_pool, pool_scale, w_out, norm_ffn, w_ffn_in, w_ffn_down, norm_final):
    B, n_real, _ = x.shape
    meta = jnp.broadcast_to(meta_tokens[None].astype(x.dtype), (B, N_META, D_MODEL))
    h_res = jnp.concatenate([meta, x], axis=1)
    L = N_META + n_real
    cos, sin = axial_rope_tables(L, n_real)

    for l in range(DEPTH):
        h = rmsnorm(h_res, norm_mix[l])
        proj = h @ w_in[l]
        q = proj[..., :D_ATTN].reshape(B, L, N_HEADS, HEAD_DIM)
        k = proj[..., D_ATTN:D_ATTN + D_KV].reshape(B, L, N_KV_HEADS, HEAD_DIM)
        v = proj[..., D_ATTN + D_KV:D_ATTN + 2 * D_KV].reshape(B, L, N_KV_HEADS, HEAD_DIM)
        u = proj[..., D_ATTN + 2 * D_KV:]

        q = apply_axial_rope(rmsnorm(q, q_norm[l]), cos, sin)
        k = apply_axial_rope(rmsnorm(k, k_norm[l]), cos, sin)
        attn = rmsnorm(gqa_bidirectional(q, k, v), attn_out_norm[l])
        pool = rmsnorm(multiscale_pool(u, w_pool[l]), pool_scale[l])

        mixed = jnp.concatenate([attn, pool], axis=-1)
        h_res = h_res + mixed @ w_out[l]

        h = rmsnorm(h_res, norm_ffn[l])
        gu = h @ w_ffn_in[l]
        h_res = h_res + (jax.nn.silu(gu[..., :D_FF]) * gu[..., D_FF:]) @ w_ffn_down[l]

    return rmsnorm(h_res, norm_final)[:, N_META:]
```

```python
import functools

import jax
import jax.numpy as jnp
from jax import lax
from jax.experimental import pallas as pl
from jax.experimental.pallas import tpu as pltpu

D_MODEL = 1024
N_META = 16
GRID_W = 64
N_HEADS = 8
N_KV_HEADS = 2
HEAD_DIM = 64
KV_GROUP = N_HEADS // N_KV_HEADS
D_ATTN = N_HEADS * HEAD_DIM
D_KV = N_KV_HEADS * HEAD_DIM
D_QK = D_ATTN + D_KV
D_POOL = 512
POOL_WINDOWS = (2, 4, 8, 16)
POOL_GROUP = 128
POOL_HALO = 8
D_IN = D_ATTN + 2 * D_KV + D_POOL
D_FF = 2816
ROPE_AXIS_PAIRS = HEAD_DIM // 4
ROPE_THETA = 10000.0
EPS = 1e-6

LANES = 128
MXU_DIM = 256
VT_ROWS = HEAD_DIM + 16
VMEM_LIMIT = 56 * 1024 * 1024

TM_PROJ = 512
TQ = 128
TM_FFN = 512
FF_CHUNK = 256

F32 = jnp.float32
BF16 = jnp.bfloat16


def _rms(x):
    return lax.rsqrt(jnp.mean(x * x, axis=-1, keepdims=True) + EPS)


def _inproj_kernel(x_ref, g_ref, w_ref, hg_ref, e_ref, cos_ref, sin_ref,
                   q_ref, k_ref, vt_ref, u_ref):
    x = x_ref[...]
    h = (x * _rms(x) * g_ref[...]).astype(BF16)
    proj = jnp.dot(h, w_ref[...], preferred_element_type=F32)

    u_ref[...] = proj[:, D_QK + D_KV:]

    vt = proj[:, D_QK:D_QK + D_KV].T
    ones = jnp.ones((VT_ROWS - HEAD_DIM, vt.shape[1]), BF16)
    for g in range(N_KV_HEADS):
        vt_ref[0, g, 0, 0:HEAD_DIM, :] = vt[g * HEAD_DIM:(g + 1) * HEAD_DIM].astype(BF16)
        vt_ref[0, g, 0, HEAD_DIM:VT_ROWS, :] = ones

    qk = proj[:, :D_QK]
    sq = (qk * qk).astype(BF16)
    e = e_ref[...]
    ss = jnp.concatenate(
        [jnp.dot(sq[:, 0:256], e, preferred_element_type=F32),
         jnp.dot(sq[:, 256:512], e, preferred_element_type=F32),
         jnp.dot(sq[:, 512:640], e[:LANES, :LANES], preferred_element_type=F32)],
        axis=1)
    qn = qk * lax.rsqrt(ss * (1.0 / HEAD_DIM) + EPS) * hg_ref[...]

    cos = cos_ref[...]
    sin = sin_ref[...]
    lane = lax.broadcasted_iota(jnp.int32, cos.shape, 1)
    first_half = (lane & ROPE_AXIS_PAIRS) == 0
    roped = []
    for c in range(D_QK // LANES):
        xc = qn[:, c * LANES:(c + 1) * LANES]
        partner = jnp.where(first_half,
                            pltpu.roll(xc, LANES - ROPE_AXIS_PAIRS, 1),
                            pltpu.roll(xc, ROPE_AXIS_PAIRS, 1))
        roped.append(xc * cos + partner * sin)
    for c in range(D_ATTN // LANES):
        q_ref[:, c * LANES:(c + 1) * LANES] = roped[c].astype(BF16)
    kc = roped[D_ATTN // LANES]
    k_ref[:, 0:LANES] = kc.astype(BF16)
    k_ref[:, LANES:2 * LANES] = pltpu.roll(kc, HEAD_DIM, 1).astype(BF16)


def _inproj(x2d, g, w, hg, e, cos, sin, n_batch, tm):
    n_rows = x2d.shape[0]
    tiles = n_rows // tm
    tpb = tiles // n_batch
    n_pos = cos.shape[0] // tm
    const = lambda i: (0, 0)
    return pl.pallas_call(
        _inproj_kernel,
        grid=(tiles,),
        in_specs=[
            pl.BlockSpec((tm, D_MODEL), lambda i: (i, 0)),
            pl.BlockSpec((1, D_MODEL), const),
            pl.BlockSpec((D_MODEL, D_IN), const),
            pl.BlockSpec((1, D_QK), const),
            pl.BlockSpec((MXU_DIM, MXU_DIM), const),
            pl.BlockSpec((tm, LANES), lambda i: (i % n_pos, 0)),
            pl.BlockSpec((tm, LANES), lambda i: (i % n_pos, 0)),
        ],
        out_specs=[
            pl.BlockSpec((tm, D_ATTN), lambda i: (i, 0)),
            pl.BlockSpec((tm, 2 * D_KV), lambda i: (i, 0)),
            pl.BlockSpec((1, N_KV_HEADS, 1, VT_ROWS, tm), lambda i: (i // tpb, 0, i % tpb, 0, 0)),
            pl.BlockSpec((tm, D_POOL), lambda i: (i, 0)),
        ],
        out_shape=[
            jax.ShapeDtypeStruct((n_rows, D_ATTN), BF16),
            jax.ShapeDtypeStruct((n_rows, 2 * D_KV), BF16),
            jax.ShapeDtypeStruct((n_batch, N_KV_HEADS, tpb, VT_ROWS, tm), BF16),
            jax.ShapeDtypeStruct((n_rows, D_POOL), F32),
        ],
        compiler_params=pltpu.CompilerParams(
            dimension_semantics=("parallel",), vmem_limit_bytes=VMEM_LIMIT),
        name="inproj",
    )(x2d, g, w, hg, e, cos, sin)


def _attn_kernel(q_ref, k_ref, vt_ref, kmeta_ref, vtmeta_ref, o_ref, m_ref, acc_ref,
                 *, tq, tk, n_chunks):
    nq = KV_GROUP * tq

    low = lax.broadcasted_iota(jnp.int32, (tq, LANES), 1) < HEAD_DIM
    heads = []
    for c in range(KV_GROUP // 2):
        col = q_ref[:, c * LANES:(c + 1) * LANES].astype(F32)
        heads.append(jnp.where(low, col, 0.0).astype(BF16))
        heads.append(jnp.where(low, pltpu.roll(col, HEAD_DIM, 1), 0.0).astype(BF16))
    qw = jnp.concatenate(heads, axis=0)

    def scores(kc):
        return lax.dot_general(kc, qw, (((1,), (1,)), ((), ())),
                               preferred_element_type=F32)

    s = scores(kmeta_ref[...])
    m0 = jnp.max(s, axis=0, keepdims=True)
    p = jnp.exp(s - m0).astype(BF16)
    p = jnp.concatenate([p, jnp.zeros((LANES - N_META, nq), BF16)], axis=0)
    m_ref[...] = m0
    acc_ref[...] = jnp.dot(vtmeta_ref[0], p, preferred_element_type=F32)

    def body(i, carry):
        s = scores(k_ref[pl.ds(pl.multiple_of(i * tk, tk), tk), :])
        m_old = m_ref[...]
        m_new = jnp.maximum(m_old, jnp.max(s, axis=0, keepdims=True))
        p = jnp.exp(s - m_new).astype(BF16)
        pv = jnp.dot(vt_ref[0, 0, i], p, preferred_element_type=F32)
        acc_ref[...] = acc_ref[...] * jnp.exp(m_old - m_new) + pv
        m_ref[...] = m_new
        return carry

    lax.fori_loop(0, n_chunks, body, 0)

    acc = acc_ref[...]
    o = acc[0:HEAD_DIM] * (1.0 / acc[HEAD_DIM:HEAD_DIM + 1])
    for c in range(KV_GROUP // 2):
        pair = jnp.concatenate([o[:, (2 * c) * tq:(2 * c + 1) * tq],
                                o[:, (2 * c + 1) * tq:(2 * c + 2) * tq]], axis=0)
        o_ref[:, c * LANES:(c + 1) * LANES] = pair.T


def _attention(q, k2, vt, kmeta2, vtmeta, n_batch, tq, tk):
    n_rows = q.shape[0]
    seq = n_rows // n_batch
    nqt = seq // tq
    n_chunks = seq // tk
    gw = KV_GROUP * HEAD_DIM
    kernel = functools.partial(_attn_kernel, tq=tq, tk=tk, n_chunks=n_chunks)
    return pl.pallas_call(
        kernel,
        grid=(n_batch, N_KV_HEADS, nqt),
        in_specs=[
            pl.BlockSpec((tq, gw), lambda b, g, i: (b * nqt + i, g)),
            pl.BlockSpec((seq, LANES), lambda b, g, i: (b, g)),
            pl.BlockSpec((1, 1, n_chunks, VT_ROWS, tk), lambda b, g, i: (b, g, 0, 0, 0)),
            pl.BlockSpec((N_META, LANES), lambda b, g, i: (0, g)),
            pl.BlockSpec((1, VT_ROWS, LANES), lambda b, g, i: (g, 0, 0)),
        ],
        out_specs=pl.BlockSpec((tq, gw), lambda b, g, i: (b * nqt + i, g)),
        out_shape=jax.ShapeDtypeStruct((n_rows, D_ATTN), F32),
        scratch_shapes=[pltpu.VMEM((1, KV_GROUP * tq), F32),
                        pltpu.VMEM((VT_ROWS, KV_GROUP * tq), F32)],
        compiler_params=pltpu.CompilerParams(
            dimension_semantics=("parallel", "parallel", "arbitrary"),
            vmem_limit_bytes=VMEM_LIMIT),
        name="attention",
    )(q, k2, vt, kmeta2, vtmeta)


def _mix_ffn_kernel(x_ref, a_ref, u_ref, uprev_ref, unext_ref, umeta_ref,
                    ga_ref, gp_ref, wp_ref, wo_ref, gf_ref, wi_ref, wd_ref, gl_ref,
                    o_ref, ext_ref, act_ref, *, tm, tpb, seq_total):
    j = pl.program_id(0) % tpb

    ext_ref[POOL_HALO:POOL_HALO + tm, :] = u_ref[...]
    ext_ref[0:POOL_HALO, :] = jnp.where(j == 0, umeta_ref[N_META - POOL_HALO:N_META, :],
                                        uprev_ref[...])
    ext_ref[POOL_HALO + tm:2 * POOL_HALO + tm, :] = jnp.where(j == tpb - 1, 0.0, unext_ref[...])

    t = j * tm + N_META + lax.broadcasted_iota(jnp.int32, (tm, 1), 0)
    diffs = []
    for gi, w in enumerate(POOL_WINDOWS):
        c0 = gi * POOL_GROUP
        win = None
        for off in range(-(w // 2), w - w // 2):
            rows = ext_ref[POOL_HALO + off:POOL_HALO + off + tm, c0:c0 + POOL_GROUP]
            win = rows if win is None else win + rows
        cnt = jnp.minimum(t + (w - w // 2), seq_total) - (t - w // 2)
        diffs.append(win / cnt.astype(F32) - ext_ref[POOL_HALO:POOL_HALO + tm, c0:c0 + POOL_GROUP])
    pooled = []
    for pr in range(2):
        d = jnp.concatenate(diffs[2 * pr:2 * pr + 2], axis=1).astype(BF16)
        pooled.append(jnp.dot(d, wp_ref[pr], preferred_element_type=F32))
    y = jnp.concatenate(pooled, axis=1)
    pool_n = y * _rms(y) * gp_ref[...]

    a = a_ref[...]
    attn_n = a * _rms(a) * ga_ref[...]

    mixed = jnp.concatenate([attn_n, pool_n], axis=1).astype(BF16)
    h1 = x_ref[...] + jnp.dot(mixed, wo_ref[...], preferred_element_type=F32)

    h2 = (h1 * _rms(h1) * gf_ref[...]).astype(BF16)
    for c in range(D_FF // FF_CHUNK):
        gu = jnp.dot(h2, wi_ref[:, 2 * c * FF_CHUNK:2 * (c + 1) * FF_CHUNK],
                     preferred_element_type=F32)
        gate = gu[:, :FF_CHUNK]
        act_ref[:, c * FF_CHUNK:(c + 1) * FF_CHUNK] = (
            gate * jax.nn.sigmoid(gate) * gu[:, FF_CHUNK:]).astype(BF16)
    h3 = h1 + jnp.dot(act_ref[...], wd_ref[...], preferred_element_type=F32)
    o_ref[...] = h3 * _rms(h3) * gl_ref[...]


def _mix_ffn(x2d, attn, u, umeta, ga, gp, wp, wo, gf, wi, wd, gl, n_batch, tm):
    n_rows = x2d.shape[0]
    tiles = n_rows // tm
    tpb = tiles // n_batch
    hb = tm // POOL_HALO
    n_hblocks = n_rows // POOL_HALO
    kernel = functools.partial(_mix_ffn_kernel, tm=tm, tpb=tpb,
                               seq_total=N_META + n_rows // n_batch)

    def const(shape):
        zeros = (0,) * len(shape)
        return pl.BlockSpec(shape, lambda i: zeros, pipeline_mode=pl.Buffered(1))

    return pl.pallas_call(
        kernel,
        grid=(tiles,),
        in_specs=[
            pl.BlockSpec((tm, D_MODEL), lambda i: (i, 0)),
            pl.BlockSpec((tm, D_ATTN), lambda i: (i, 0)),
            pl.BlockSpec((tm, D_POOL), lambda i: (i, 0)),
            pl.BlockSpec((POOL_HALO, D_POOL), lambda i: (jnp.maximum(i * hb - 1, 0), 0)),
            pl.BlockSpec((POOL_HALO, D_POOL),
                         lambda i: (jnp.minimum((i + 1) * hb, n_hblocks - 1), 0)),
            const((N_META, D_POOL)),
            const((1, D_ATTN)),
            const((1, D_POOL)),
            const((2, MXU_DIM, MXU_DIM)),
            const((D_MODEL, D_MODEL)),
            const((1, D_MODEL)),
            const((D_MODEL, 2 * D_FF)),
            const((D_FF, D_MODEL)),
            const((1, D_MODEL)),
        ],
        out_specs=pl.BlockSpec((tm, D_MODEL), lambda i: (i, 0)),
        out_shape=jax.ShapeDtypeStruct((n_rows, D_MODEL), F32),
        scratch_shapes=[pltpu.VMEM((tm + 2 * POOL_HALO, D_POOL), F32),
                        pltpu.VMEM((tm, D_FF), BF16)],
        compiler_params=pltpu.CompilerParams(
            dimension_semantics=("parallel",), vmem_limit_bytes=VMEM_LIMIT),
        name="mix_ffn",
    )(x2d, attn, u, u, u, umeta, ga, gp, wp, wo, gf, wi, wd, gl)


def _rope_tables(n_real):
    rows = n_real // GRID_W
    r = jnp.arange(n_real, dtype=jnp.int32)
    row = (r // GRID_W - rows // 2).astype(F32)
    col = (r % GRID_W - GRID_W // 2).astype(F32)
    freqs = ROPE_THETA ** (-jnp.arange(ROPE_AXIS_PAIRS, dtype=F32) / ROPE_AXIS_PAIRS)
    ar = row[:, None] * freqs
    ac = col[:, None] * freqs
    cos = jnp.concatenate([jnp.cos(ar), jnp.cos(ar), jnp.cos(ac), jnp.cos(ac)], axis=1)
    sin = jnp.concatenate([-jnp.sin(ar), jnp.sin(ar), -jnp.sin(ac), jnp.sin(ac)], axis=1)
    return jnp.tile(cos, (1, 2)), jnp.tile(sin, (1, 2))


def kernel(x, meta_tokens, norm_mix, w_in, q_norm, k_norm, attn_out_norm, w_pool,
           pool_scale, w_out, norm_ffn, w_ffn_in, w_ffn_down, norm_final):
    n_batch, n_real, _ = x.shape
    assert n_real % TM_PROJ == 0 and n_real % TQ == 0 and n_real % TM_FFN == 0
    assert w_in.shape[0] == 1, "single layer"
    x2d = x.reshape(n_batch * n_real, D_MODEL)

    w_in_b = w_in[0].astype(BF16)
    g_mix = norm_mix[0].reshape(1, D_MODEL)
    head_gain = jnp.concatenate([jnp.tile(q_norm[0], N_HEADS) * (HEAD_DIM ** -0.5),
                                 jnp.tile(k_norm[0], N_KV_HEADS)]).reshape(1, D_QK)
    blk = jnp.arange(MXU_DIM) // HEAD_DIM
    e_heads = (blk[:, None] == blk[None, :]).astype(BF16)
    cos, sin = _rope_tables(n_real)

    wp = jnp.zeros((2, MXU_DIM, MXU_DIM), F32)
    for gi in range(len(POOL_WINDOWS)):
        r0 = (gi % 2) * POOL_GROUP
        wp = wp.at[gi // 2, r0:r0 + POOL_GROUP, r0:r0 + POOL_GROUP].set(w_pool[0, gi])
    wp = wp.astype(BF16)
    w_out_b = w_out[0].astype(BF16)
    n_ffc = D_FF // FF_CHUNK
    wi = w_ffn_in[0].reshape(D_MODEL, 2, n_ffc, FF_CHUNK).transpose(0, 2, 1, 3)
    wi = wi.reshape(D_MODEL, 2 * D_FF).astype(BF16)
    wd = w_ffn_down[0].astype(BF16)

    q, k2, vt, u = _inproj(x2d, g_mix, w_in_b, head_gain, e_heads, cos, sin, n_batch, TM_PROJ)
    meta_pad = jnp.zeros((LANES, D_MODEL), F32).at[:N_META].set(meta_tokens)
    ident_cos = jnp.ones((LANES, LANES), F32)
    ident_sin = jnp.zeros((LANES, LANES), F32)
    _, kmeta2, vtmeta, umeta = _inproj(meta_pad, g_mix, w_in_b, head_gain, e_heads,
                                       ident_cos, ident_sin, 1, LANES)
    kmeta2 = kmeta2[:N_META]
    lane = jnp.arange(LANES)
    vtmeta = jnp.where(lane[None, None, :] < N_META, vtmeta[0, :, 0], 0).astype(BF16)
    umeta = umeta[:N_META]

    attn = _attention(q, k2, vt, kmeta2, vtmeta, n_batch, TQ, TM_PROJ)

    out = _mix_ffn(x2d, attn, u, umeta,
                   attn_out_norm[0].reshape(1, D_ATTN), pool_scale[0].reshape(1, D_POOL),
                   wp, w_out_b, norm_ffn[0].reshape(1, D_MODEL), wi, wd,
                   norm_final.reshape(1, D_MODEL), n_batch, TM_FFN)
    return out.reshape(n_batch, n_real, D_MODEL)
```

```python
import functools

import jax
import jax.numpy as jnp
from jax import lax
from jax.experimental import pallas as pl
from jax.experimental.pallas import tpu as pltpu

D_MODEL = 1024
N_META = 16
GRID_W = 64
N_HEADS = 8
N_KV_HEADS = 2
HEAD_DIM = 64
KV_GROUP = N_HEADS // N_KV_HEADS
D_ATTN = N_HEADS * HEAD_DIM
D_KV = N_KV_HEADS * HEAD_DIM
D_QK = D_ATTN + D_KV
D_POOL = 512
POOL_WINDOWS = (2, 4, 8, 16)
POOL_GROUP = 128
POOL_HALO = 8
D_IN = D_ATTN + 2 * D_KV + D_POOL
D_FF = 2816
ROPE_AXIS_PAIRS = HEAD_DIM // 4
ROPE_THETA = 10000.0
EPS = 1e-6

LANES = 128
MXU_DIM = 256
VT_ROWS = HEAD_DIM + 16
VMEM_LIMIT = 56 * 1024 * 1024

LOG2_E = 1.4426950408889634
NEG_BIG = -1e30

TM_PROJ = 512
TQ = 256
TK = 512
TM_FFN = 512
FF_CHUNK = 256

F32 = jnp.float32
BF16 = jnp.bfloat16


def _rms(x):
    return lax.rsqrt(jnp.mean(x * x, axis=-1, keepdims=True) + EPS)


def _inproj_kernel(x_ref, g_ref, w_ref, hg_ref, e_ref, cos_ref, sin_ref,
                   qt_ref, k_ref, vt_ref, u_ref):
    x = x_ref[...]
    h = (x * _rms(x) * g_ref[...]).astype(BF16)
    proj = jnp.dot(h, w_ref[...], preferred_element_type=F32)

    u_ref[...] = proj[:, D_QK + D_KV:]

    vt = proj[:, D_QK:D_QK + D_KV].T
    ones = jnp.ones((VT_ROWS - HEAD_DIM, vt.shape[1]), BF16)
    for g in range(N_KV_HEADS):
        vt_ref[0, g, 0:HEAD_DIM, :] = vt[g * HEAD_DIM:(g + 1) * HEAD_DIM].astype(BF16)
        vt_ref[0, g, HEAD_DIM:VT_ROWS, :] = ones

    qk = proj[:, :D_QK]
    sq = (qk * qk).astype(BF16)
    e = e_ref[...]
    ss = jnp.concatenate(
        [jnp.dot(sq[:, 0:256], e, preferred_element_type=F32),
         jnp.dot(sq[:, 256:512], e, preferred_element_type=F32),
         jnp.dot(sq[:, 512:640], e[:LANES, :LANES], preferred_element_type=F32)],
        axis=1)
    qn = qk * lax.rsqrt(ss * (1.0 / HEAD_DIM) + EPS) * hg_ref[...]

    cos = cos_ref[...]
    sin = sin_ref[...]
    lane = lax.broadcasted_iota(jnp.int32, cos.shape, 1)
    first_half = (lane & ROPE_AXIS_PAIRS) == 0
    roped = []
    for c in range(D_QK // LANES):
        xc = qn[:, c * LANES:(c + 1) * LANES]
        partner = jnp.where(first_half,
                            pltpu.roll(xc, LANES - ROPE_AXIS_PAIRS, 1),
                            pltpu.roll(xc, ROPE_AXIS_PAIRS, 1))
        roped.append(xc * cos + partner * sin)
    tq = qt_ref.shape[-1]
    for c in range(D_ATTN // LANES):
        qt = roped[c].T.astype(BF16)
        for t in range(qt_ref.shape[2]):
            qt_ref[0, c, t] = qt[:, t * tq:(t + 1) * tq]
    kc = roped[D_ATTN // LANES]
    k_ref[:, 0:LANES] = kc.astype(BF16)
    k_ref[:, LANES:2 * LANES] = pltpu.roll(kc, HEAD_DIM, 1).astype(BF16)


def _inproj(x2d, g, w, hg, e, cos, sin, n_batch, tm, tq):
    n_rows = x2d.shape[0]
    tiles = n_rows // tm
    tpb = tiles // n_batch
    n_pos = cos.shape[0] // tm
    qpt = tm // tq
    n_pairs = N_HEADS // 2
    const = lambda i: (0, 0)
    return pl.pallas_call(
        _inproj_kernel,
        grid=(tiles,),
        in_specs=[
            pl.BlockSpec((tm, D_MODEL), lambda i: (i, 0)),
            pl.BlockSpec((1, D_MODEL), const),
            pl.BlockSpec((D_MODEL, D_IN), const),
            pl.BlockSpec((1, D_QK), const),
            pl.BlockSpec((MXU_DIM, MXU_DIM), const),
            pl.BlockSpec((tm, LANES), lambda i: (i % n_pos, 0)),
            pl.BlockSpec((tm, LANES), lambda i: (i % n_pos, 0)),
        ],
        out_specs=[
            pl.BlockSpec((1, n_pairs, qpt, LANES, tq), lambda i: (i // tpb, 0, i % tpb, 0, 0)),
            pl.BlockSpec((tm, 2 * D_KV), lambda i: (i, 0)),
            pl.BlockSpec((1, N_KV_HEADS, VT_ROWS, tm), lambda i: (i // tpb, 0, 0, i % tpb)),
            pl.BlockSpec((tm, D_POOL), lambda i: (i, 0)),
        ],
        out_shape=[
            jax.ShapeDtypeStruct((n_batch, n_pairs, tpb * qpt, LANES, tq), BF16),
            jax.ShapeDtypeStruct((n_rows, 2 * D_KV), BF16),
            jax.ShapeDtypeStruct((n_batch, N_KV_HEADS, VT_ROWS, tpb * tm), BF16),
            jax.ShapeDtypeStruct((n_rows, D_POOL), F32),
        ],
        compiler_params=pltpu.CompilerParams(
            dimension_semantics=("parallel",), vmem_limit_bytes=VMEM_LIMIT),
        name="inproj",
    )(x2d, g, w, hg, e, cos, sin)


def _attn_kernel(q_ref, k_ref, vt_ref, kmeta_ref, vtmeta_ref, o_ref, kx_ref, vx_ref, s_ref,
                 *, tq, tk, seq):
    n_qt = seq // tq
    nq = 2 * tq
    n_chunks = seq // tk
    kx_rows = seq + LANES
    bounds = [(c * tk, (c + 1) * tk) for c in range(n_chunks)]
    bounds[-1] = (bounds[-1][0], kx_rows)

    kx_ref[0:seq, :] = k_ref[...]
    kx_ref[seq:seq + N_META, :] = kmeta_ref[...]
    kx_ref[seq + N_META:kx_rows, :] = jnp.zeros((LANES - N_META, LANES), BF16)
    vx_ref[:, 0:seq] = vt_ref[0, 0]
    vx_ref[:, seq:kx_rows] = vtmeta_ref[0]
    pad_row = lax.broadcasted_iota(jnp.int32, (LANES, nq), 0) >= N_META

    def weights(j, pair):
        qt = q_ref[0, pair, j]
        z = jnp.zeros((HEAD_DIM, tq), BF16)
        return jnp.concatenate([jnp.concatenate([qt[0:HEAD_DIM], z], axis=0),
                                jnp.concatenate([qt[HEAD_DIM:], z], axis=0)], axis=1)

    def score_chunk(w, slot, c, mx):
        lo, hi = bounds[c]
        s = jnp.dot(kx_ref[lo:hi, :], w, preferred_element_type=F32)
        if hi == kx_rows:
            s = jnp.concatenate([s[:seq - lo], jnp.where(pad_row, NEG_BIG, s[seq - lo:])], axis=0)
        s_ref[slot, lo:hi, :] = s
        return jnp.maximum(mx, jnp.max(s, axis=0, keepdims=True))

    def pv_chunk(slot, c, m, acc):
        lo, hi = bounds[c]
        p = jnp.exp2(s_ref[slot, lo:hi, :] - m).astype(BF16)
        return acc + jnp.dot(vx_ref[:, lo:hi], p, preferred_element_type=F32)

    def stage(w_next, slot_next, slot_cur, m_cur):
        mx = jnp.full((1, nq), NEG_BIG, F32)
        acc = jnp.zeros((VT_ROWS, nq), F32)
        for c in range(n_chunks):
            mx = score_chunk(w_next, slot_next, c, mx)
            acc = pv_chunk(slot_cur, c, m_cur, acc)
        return mx, acc

    def finish(acc, j, pair):
        o = acc[0:HEAD_DIM] * (1.0 / acc[HEAD_DIM:HEAD_DIM + 1])
        both = jnp.concatenate([o[:, :tq], o[:, tq:]], axis=0)
        o_ref[pl.ds(pl.multiple_of(j * tq, tq), tq), pair * LANES:(pair + 1) * LANES] = both.T

    w0 = weights(0, 0)
    m0 = jnp.full((1, nq), NEG_BIG, F32)
    for c in range(n_chunks):
        m0 = score_chunk(w0, 0, c, m0)

    def body(j, m_a):
        m_b, acc = stage(weights(j, 1), 1, 0, m_a)
        finish(acc, j, 0)
        m_a, acc = stage(weights(jnp.minimum(j + 1, n_qt - 1), 0), 0, 1, m_b)
        finish(acc, j, 1)
        return m_a

    lax.fori_loop(0, n_qt, body, m0)


def _attention(qt, k2, vt, kmeta2, vtmeta, n_batch, tq, tk):
    n_rows = k2.shape[0]
    seq = n_rows // n_batch
    n_qt = seq // tq
    gw = KV_GROUP * HEAD_DIM
    kernel = functools.partial(_attn_kernel, tq=tq, tk=tk, seq=seq)
    return pl.pallas_call(
        kernel,
        grid=(n_batch, N_KV_HEADS),
        in_specs=[
            pl.BlockSpec((1, KV_GROUP // 2, n_qt, LANES, tq), lambda b, g: (b, g, 0, 0, 0)),
            pl.BlockSpec((seq, LANES), lambda b, g: (b, g)),
            pl.BlockSpec((1, 1, VT_ROWS, seq), lambda b, g: (b, g, 0, 0)),
            pl.BlockSpec((N_META, LANES), lambda b, g: (0, g)),
            pl.BlockSpec((1, VT_ROWS, LANES), lambda b, g: (g, 0, 0)),
        ],
        out_specs=pl.BlockSpec((seq, gw), lambda b, g: (b, g)),
        out_shape=jax.ShapeDtypeStruct((n_rows, D_ATTN), F32),
        scratch_shapes=[pltpu.VMEM((seq + LANES, LANES), BF16),
                        pltpu.VMEM((VT_ROWS, seq + LANES), BF16),
                        pltpu.VMEM((2, seq + LANES, 2 * tq), F32)],
        compiler_params=pltpu.CompilerParams(
            dimension_semantics=("parallel", "parallel"), vmem_limit_bytes=VMEM_LIMIT),
        name="attention",
    )(qt, k2, vt, kmeta2, vtmeta)


def _mix_ffn_kernel(x_ref, a_ref, u_ref, uprev_ref, unext_ref, umeta_ref,
                    ga_ref, gp_ref, wp_ref, wo_ref, gf_ref, wi_ref, wd_ref, gl_ref,
                    o_ref, ext_ref, act_ref, *, tm, tpb, seq_total):
    j = pl.program_id(0) % tpb

    ext_ref[POOL_HALO:POOL_HALO + tm, :] = u_ref[...]
    ext_ref[0:POOL_HALO, :] = jnp.where(j == 0, umeta_ref[N_META - POOL_HALO:N_META, :],
                                        uprev_ref[...])
    ext_ref[POOL_HALO + tm:2 * POOL_HALO + tm, :] = jnp.where(j == tpb - 1, 0.0, unext_ref[...])

    t = j * tm + N_META + lax.broadcasted_iota(jnp.int32, (tm, 1), 0)
    diffs = []
    for gi, w in enumerate(POOL_WINDOWS):
        c0 = gi * POOL_GROUP
        win = None
        for off in range(-(w // 2), w - w // 2):
            rows = ext_ref[POOL_HALO + off:POOL_HALO + off + tm, c0:c0 + POOL_GROUP]
            win = rows if win is None else win + rows
        cnt = jnp.minimum(t + (w - w // 2), seq_total) - (t - w // 2)
        diffs.append(win / cnt.astype(F32) - ext_ref[POOL_HALO:POOL_HALO + tm, c0:c0 + POOL_GROUP])
    pooled = []
    for pr in range(2):
        d = jnp.concatenate(diffs[2 * pr:2 * pr + 2], axis=1).astype(BF16)
        pooled.append(jnp.dot(d, wp_ref[pr], preferred_element_type=F32))
    y = jnp.concatenate(pooled, axis=1)
    pool_n = y * _rms(y) * gp_ref[...]

    a = a_ref[...]
    attn_n = a * _rms(a) * ga_ref[...]

    mixed = jnp.concatenate([attn_n, pool_n], axis=1).astype(BF16)
    h1 = x_ref[...] + jnp.dot(mixed, wo_ref[...], preferred_element_type=F32)

    h2 = (h1 * _rms(h1) * gf_ref[...]).astype(BF16)
    for c in range(D_FF // FF_CHUNK):
        gu = jnp.dot(h2, wi_ref[:, 2 * c * FF_CHUNK:2 * (c + 1) * FF_CHUNK],
                     preferred_element_type=F32)
        gate = gu[:, :FF_CHUNK]
        act_ref[:, c * FF_CHUNK:(c + 1) * FF_CHUNK] = (
            gate * jax.nn.sigmoid(gate) * gu[:, FF_CHUNK:]).astype(BF16)
    h3 = h1 + jnp.dot(act_ref[...], wd_ref[...], preferred_element_type=F32)
    o_ref[...] = h3 * _rms(h3) * gl_ref[...]


def _mix_ffn(x2d, attn, u, umeta, ga, gp, wp, wo, gf, wi, wd, gl, n_batch, tm):
    n_rows = x2d.shape[0]
    tiles = n_rows // tm
    tpb = tiles // n_batch
    hb = tm // POOL_HALO
    n_hblocks = n_rows // POOL_HALO
    kernel = functools.partial(_mix_ffn_kernel, tm=tm, tpb=tpb,
                               seq_total=N_META + n_rows // n_batch)

    def const(shape):
        zeros = (0,) * len(shape)
        return pl.BlockSpec(shape, lambda i: zeros, pipeline_mode=pl.Buffered(1))

    return pl.pallas_call(
        kernel,
        grid=(tiles,),
        in_specs=[
            pl.BlockSpec((tm, D_MODEL), lambda i: (i, 0)),
            pl.BlockSpec((tm, D_ATTN), lambda i: (i, 0)),
            pl.BlockSpec((tm, D_POOL), lambda i: (i, 0)),
            pl.BlockSpec((POOL_HALO, D_POOL), lambda i: (jnp.maximum(i * hb - 1, 0), 0)),
            pl.BlockSpec((POOL_HALO, D_POOL),
                         lambda i: (jnp.minimum((i + 1) * hb, n_hblocks - 1), 0)),
            const((N_META, D_POOL)),
            const((1, D_ATTN)),
            const((1, D_POOL)),
            const((2, MXU_DIM, MXU_DIM)),
            const((D_MODEL, D_MODEL)),
            const((1, D_MODEL)),
            const((D_MODEL, 2 * D_FF)),
            const((D_FF, D_MODEL)),
            const((1, D_MODEL)),
        ],
        out_specs=pl.BlockSpec((tm, D_MODEL), lambda i: (i, 0)),
        out_shape=jax.ShapeDtypeStruct((n_rows, D_MODEL), F32),
        scratch_shapes=[pltpu.VMEM((tm + 2 * POOL_HALO, D_POOL), F32),
                        pltpu.VMEM((tm, D_FF), BF16)],
        compiler_params=pltpu.CompilerParams(
            dimension_semantics=("parallel",), vmem_limit_bytes=VMEM_LIMIT),
        name="mix_ffn",
    )(x2d, attn, u, u, u, umeta, ga, gp, wp, wo, gf, wi, wd, gl)


def _rope_tables(n_real):
    rows = n_real // GRID_W
    r = jnp.arange(n_real, dtype=jnp.int32)
    row = (r // GRID_W - rows // 2).astype(F32)
    col = (r % GRID_W - GRID_W // 2).astype(F32)
    freqs = ROPE_THETA ** (-jnp.arange(ROPE_AXIS_PAIRS, dtype=F32) / ROPE_AXIS_PAIRS)
    ar = row[:, None] * freqs
    ac = col[:, None] * freqs
    cos = jnp.concatenate([jnp.cos(ar), jnp.cos(ar), jnp.cos(ac), jnp.cos(ac)], axis=1)
    sin = jnp.concatenate([-jnp.sin(ar), jnp.sin(ar), -jnp.sin(ac), jnp.sin(ac)], axis=1)
    return jnp.tile(cos, (1, 2)), jnp.tile(sin, (1, 2))


def kernel(x, meta_tokens, norm_mix, w_in, q_norm, k_norm, attn_out_norm, w_pool,
           pool_scale, w_out, norm_ffn, w_ffn_in, w_ffn_down, norm_final):
    n_batch, n_real, _ = x.shape
    assert n_real % TM_PROJ == 0 and n_real % TQ == 0 and n_real % TM_FFN == 0
    assert w_in.shape[0] == 1, "single layer"
    x2d = x.reshape(n_batch * n_real, D_MODEL)

    w_in_b = w_in[0].astype(BF16)
    g_mix = norm_mix[0].reshape(1, D_MODEL)
    head_gain = jnp.concatenate([jnp.tile(q_norm[0], N_HEADS) * (HEAD_DIM ** -0.5 * LOG2_E),
                                 jnp.tile(k_norm[0], N_KV_HEADS)]).reshape(1, D_QK)
    blk = jnp.arange(MXU_DIM) // HEAD_DIM
    e_heads = (blk[:, None] == blk[None, :]).astype(BF16)
    cos, sin = _rope_tables(n_real)

    wp = jnp.zeros((2, MXU_DIM, MXU_DIM), F32)
    for gi in range(len(POOL_WINDOWS)):
        r0 = (gi % 2) * POOL_GROUP
        wp = wp.at[gi // 2, r0:r0 + POOL_GROUP, r0:r0 + POOL_GROUP].set(w_pool[0, gi])
    wp = wp.astype(BF16)
    w_out_b = w_out[0].astype(BF16)
    n_ffc = D_FF // FF_CHUNK
    wi = w_ffn_in[0].reshape(D_MODEL, 2, n_ffc, FF_CHUNK).transpose(0, 2, 1, 3)
    wi = wi.reshape(D_MODEL, 2 * D_FF).astype(BF16)
    wd = w_ffn_down[0].astype(BF16)

    qt, k2, vt, u = _inproj(x2d, g_mix, w_in_b, head_gain, e_heads, cos, sin, n_batch,
                            TM_PROJ, TQ)
    meta_pad = jnp.zeros((TQ, D_MODEL), F32).at[:N_META].set(meta_tokens)
    ident_cos = jnp.ones((TQ, LANES), F32)
    ident_sin = jnp.zeros((TQ, LANES), F32)
    _, kmeta2, vtmeta, umeta = _inproj(meta_pad, g_mix, w_in_b, head_gain, e_heads,
                                       ident_cos, ident_sin, 1, TQ, TQ)
    kmeta2 = kmeta2[:N_META]
    lane = jnp.arange(LANES)
    vtmeta = jnp.where(lane[None, None, :] < N_META, vtmeta[0, :, :, :LANES], 0).astype(BF16)
    umeta = umeta[:N_META]

    attn = _attention(qt, k2, vt, kmeta2, vtmeta, n_batch, TQ, TK)

    out = _mix_ffn(x2d, attn, u, umeta,
                   attn_out_norm[0].reshape(1, D_ATTN), pool_scale[0].reshape(1, D_POOL),
                   wp, w_out_b, norm_ffn[0].reshape(1, D_MODEL), wi, wd,
                   norm_final.reshape(1, D_MODEL), n_batch, TM_FFN)
    return out.reshape(n_batch, n_real, D_MODEL)
```

```python
import functools

import jax
import jax.numpy as jnp
from jax import lax
from jax.experimental import pallas as pl
from jax.experimental.pallas import tpu as pltpu

D_MODEL = 1024
N_META = 16
GRID_W = 64
N_HEADS = 8
N_KV_HEADS = 2
HEAD_DIM = 64
KV_GROUP = N_HEADS // N_KV_HEADS
D_ATTN = N_HEADS * HEAD_DIM
D_KV = N_KV_HEADS * HEAD_DIM
D_QK = D_ATTN + D_KV
D_POOL = 512
POOL_WINDOWS = (2, 4, 8, 16)
POOL_GROUP = 128
POOL_HALO = 8
D_IN = D_ATTN + 2 * D_KV + D_POOL
D_FF = 2816
ROPE_AXIS_PAIRS = HEAD_DIM // 4
ROPE_THETA = 10000.0
EPS = 1e-6

LANES = 128
MXU_DIM = 256
VT_ROWS = HEAD_DIM + 16
VMEM_LIMIT = 56 * 1024 * 1024

LOG2_E = 1.4426950408889634
NEG_BIG = -1e30

TM_PROJ = 512
TQ = 256
TK = 256
TM_FFN = 512
FF_CHUNK = 256

F32 = jnp.float32
BF16 = jnp.bfloat16


def _rms(x):
    return lax.rsqrt(jnp.mean(x * x, axis=-1, keepdims=True) + EPS)


def _inproj_kernel(x_ref, g_ref, w_ref, hg_ref, e_ref, cos_ref, sin_ref,
                   qt_ref, k_ref, vt_ref, u_ref):
    x = x_ref[...]
    h = (x * _rms(x) * g_ref[...]).astype(BF16)
    proj = jnp.dot(h, w_ref[...], preferred_element_type=F32)

    u_ref[...] = proj[:, D_QK + D_KV:]

    vt = proj[:, D_QK:D_QK + D_KV].T
    ones = jnp.ones((VT_ROWS - HEAD_DIM, vt.shape[1]), BF16)
    for g in range(N_KV_HEADS):
        vt_ref[0, g, 0:HEAD_DIM, :] = vt[g * HEAD_DIM:(g + 1) * HEAD_DIM].astype(BF16)
        vt_ref[0, g, HEAD_DIM:VT_ROWS, :] = ones

    qk = proj[:, :D_QK]
    sq = (qk * qk).astype(BF16)
    e = e_ref[...]
    ss = jnp.concatenate(
        [jnp.dot(sq[:, 0:256], e, preferred_element_type=F32),
         jnp.dot(sq[:, 256:512], e, preferred_element_type=F32),
         jnp.dot(sq[:, 512:640], e[:LANES, :LANES], preferred_element_type=F32)],
        axis=1)
    qn = qk * lax.rsqrt(ss * (1.0 / HEAD_DIM) + EPS) * hg_ref[...]

    cos = cos_ref[...]
    sin = sin_ref[...]
    lane = lax.broadcasted_iota(jnp.int32, cos.shape, 1)
    first_half = (lane & ROPE_AXIS_PAIRS) == 0
    roped = []
    for c in range(D_QK // LANES):
        xc = qn[:, c * LANES:(c + 1) * LANES]
        partner = jnp.where(first_half,
                            pltpu.roll(xc, LANES - ROPE_AXIS_PAIRS, 1),
                            pltpu.roll(xc, ROPE_AXIS_PAIRS, 1))
        roped.append(xc * cos + partner * sin)
    tq = qt_ref.shape[-1]
    for c in range(D_ATTN // LANES):
        qt = roped[c].T.astype(BF16)
        for t in range(qt_ref.shape[2]):
            qt_ref[0, c, t] = qt[:, t * tq:(t + 1) * tq]
    kc = roped[D_ATTN // LANES]
    k_ref[:, 0:LANES] = kc.astype(BF16)
    k_ref[:, LANES:2 * LANES] = pltpu.roll(kc, HEAD_DIM, 1).astype(BF16)


def _inproj(x2d, g, w, hg, e, cos, sin, n_batch, tm, tq):
    n_rows = x2d.shape[0]
    tiles = n_rows // tm
    tpb = tiles // n_batch
    n_pos = cos.shape[0] // tm
    qpt = tm // tq
    n_pairs = N_HEADS // 2
    const = lambda i: (0, 0)
    return pl.pallas_call(
        _inproj_kernel,
        grid=(tiles,),
        in_specs=[
            pl.BlockSpec((tm, D_MODEL), lambda i: (i, 0)),
            pl.BlockSpec((1, D_MODEL), const),
            pl.BlockSpec((D_MODEL, D_IN), const),
            pl.BlockSpec((1, D_QK), const),
            pl.BlockSpec((MXU_DIM, MXU_DIM), const),
            pl.BlockSpec((tm, LANES), lambda i: (i % n_pos, 0)),
            pl.BlockSpec((tm, LANES), lambda i: (i % n_pos, 0)),
        ],
        out_specs=[
            pl.BlockSpec((1, n_pairs, qpt, LANES, tq), lambda i: (i // tpb, 0, i % tpb, 0, 0)),
            pl.BlockSpec((tm, 2 * D_KV), lambda i: (i, 0)),
            pl.BlockSpec((1, N_KV_HEADS, VT_ROWS, tm), lambda i: (i // tpb, 0, 0, i % tpb)),
            pl.BlockSpec((tm, D_POOL), lambda i: (i, 0)),
        ],
        out_shape=[
            jax.ShapeDtypeStruct((n_batch, n_pairs, tpb * qpt, LANES, tq), BF16),
            jax.ShapeDtypeStruct((n_rows, 2 * D_KV), BF16),
            jax.ShapeDtypeStruct((n_batch, N_KV_HEADS, VT_ROWS, tpb * tm), BF16),
            jax.ShapeDtypeStruct((n_rows, D_POOL), F32),
        ],
        compiler_params=pltpu.CompilerParams(
            dimension_semantics=("parallel",), vmem_limit_bytes=VMEM_LIMIT),
        name="inproj",
    )(x2d, g, w, hg, e, cos, sin)


def _attn_kernel(q_ref, k_ref, vt_ref, kmeta_ref, vtmeta_ref, o_ref, kx_ref, vx_ref, s_ref,
                 *, tq, tk, seq):
    n_qt = seq // tq
    nq = 2 * tq
    n_chunks = seq // tk
    n_keys = seq + N_META
    bounds = [(c * tk, (c + 1) * tk) for c in range(n_chunks)]
    bounds[-1] = (bounds[-1][0], n_keys)

    kx_ref[0:seq, :] = k_ref[...]
    kx_ref[seq:n_keys, :] = kmeta_ref[...]
    vx_ref[:, 0:seq] = vt_ref[0, 0]
    vx_ref[:, seq:seq + LANES] = vtmeta_ref[0]

    def weights(j, pair):
        qt = q_ref[0, pair, j]
        z = jnp.zeros((HEAD_DIM, tq), BF16)
        return jnp.concatenate([jnp.concatenate([qt[0:HEAD_DIM], z], axis=0),
                                jnp.concatenate([qt[HEAD_DIM:], z], axis=0)], axis=1)

    def score_chunk(w, slot, c, mx):
        lo, hi = bounds[c]
        s = jnp.dot(kx_ref[lo:hi, :], w, preferred_element_type=F32)
        s_ref[slot, lo:hi, :] = s
        return jnp.maximum(mx, jnp.max(s, axis=0, keepdims=True))

    def pv_chunk(slot, c, m, acc):
        lo, hi = bounds[c]
        p = jnp.exp2(s_ref[slot, lo:hi, :] - m).astype(BF16)
        if hi == n_keys:
            hi = seq + LANES
            p = jnp.concatenate([p, jnp.zeros((hi - n_keys, nq), BF16)], axis=0)
        return acc + jnp.dot(vx_ref[:, lo:hi], p, preferred_element_type=F32)

    def stage(w_next, slot_next, slot_cur, m_cur):
        mx = jnp.full((1, nq), NEG_BIG, F32)
        acc = jnp.zeros((VT_ROWS, nq), F32)
        for c in range(n_chunks):
            mx = score_chunk(w_next, slot_next, c, mx)
            acc = pv_chunk(slot_cur, c, m_cur, acc)
        return mx, acc

    def finish(acc, j, pair):
        o = acc[0:HEAD_DIM] * (1.0 / acc[HEAD_DIM:HEAD_DIM + 1])
        both = jnp.concatenate([o[:, :tq], o[:, tq:]], axis=0)
        o_ref[pl.ds(pl.multiple_of(j * tq, tq), tq), pair * LANES:(pair + 1) * LANES] = both.T

    w0 = weights(0, 0)
    m0 = jnp.full((1, nq), NEG_BIG, F32)
    for c in range(n_chunks):
        m0 = score_chunk(w0, 0, c, m0)

    def body(j, m_a):
        m_b, acc = stage(weights(j, 1), 1, 0, m_a)
        finish(acc, j, 0)
        m_a, acc = stage(weights(jnp.minimum(j + 1, n_qt - 1), 0), 0, 1, m_b)
        finish(acc, j, 1)
        return m_a

    lax.fori_loop(0, n_qt, body, m0)


def _attention(qt, k2, vt, kmeta2, vtmeta, n_batch, tq, tk):
    n_rows = k2.shape[0]
    seq = n_rows // n_batch
    n_qt = seq // tq
    gw = KV_GROUP * HEAD_DIM
    kernel = functools.partial(_attn_kernel, tq=tq, tk=tk, seq=seq)
    return pl.pallas_call(
        kernel,
        grid=(n_batch, N_KV_HEADS),
        in_specs=[
            pl.BlockSpec((1, KV_GROUP // 2, n_qt, LANES, tq), lambda b, g: (b, g, 0, 0, 0)),
            pl.BlockSpec((seq, LANES), lambda b, g: (b, g)),
            pl.BlockSpec((1, 1, VT_ROWS, seq), lambda b, g: (b, g, 0, 0)),
            pl.BlockSpec((N_META, LANES), lambda b, g: (0, g)),
            pl.BlockSpec((1, VT_ROWS, LANES), lambda b, g: (g, 0, 0)),
        ],
        out_specs=pl.BlockSpec((seq, gw), lambda b, g: (b, g)),
        out_shape=jax.ShapeDtypeStruct((n_rows, D_ATTN), F32),
        scratch_shapes=[pltpu.VMEM((seq + LANES, LANES), BF16),
                        pltpu.VMEM((VT_ROWS, seq + LANES), BF16),
                        pltpu.VMEM((2, seq + LANES, 2 * tq), F32)],
        compiler_params=pltpu.CompilerParams(
            dimension_semantics=("parallel", "parallel"), vmem_limit_bytes=VMEM_LIMIT),
        name="attention",
    )(qt, k2, vt, kmeta2, vtmeta)


def _mix_ffn_kernel(x_ref, a_ref, u_ref, uprev_ref, unext_ref, umeta_ref,
                    ga_ref, gp_ref, wp_ref, wo_ref, gf_ref, wi_ref, wd_ref, gl_ref,
                    o_ref, ext_ref, act_ref, *, tm, tpb, seq_total):
    j = pl.program_id(0) % tpb

    ext_ref[POOL_HALO:POOL_HALO + tm, :] = u_ref[...]
    ext_ref[0:POOL_HALO, :] = jnp.where(j == 0, umeta_ref[N_META - POOL_HALO:N_META, :],
                                        uprev_ref[...])
    ext_ref[POOL_HALO + tm:2 * POOL_HALO + tm, :] = jnp.where(j == tpb - 1, 0.0, unext_ref[...])

    t = j * tm + N_META + lax.broadcasted_iota(jnp.int32, (tm, 1), 0)
    diffs = []
    for gi, w in enumerate(POOL_WINDOWS):
        c0 = gi * POOL_GROUP
        win = None
        for off in range(-(w // 2), w - w // 2):
            rows = ext_ref[POOL_HALO + off:POOL_HALO + off + tm, c0:c0 + POOL_GROUP]
            win = rows if win is None else win + rows
        cnt = jnp.minimum(t + (w - w // 2), seq_total) - (t - w // 2)
        diffs.append(win / cnt.astype(F32) - ext_ref[POOL_HALO:POOL_HALO + tm, c0:c0 + POOL_GROUP])
    pooled = []
    for pr in range(2):
        d = jnp.concatenate(diffs[2 * pr:2 * pr + 2], axis=1).astype(BF16)
        pooled.append(jnp.dot(d, wp_ref[pr], preferred_element_type=F32))
    y = jnp.concatenate(pooled, axis=1)
    pool_n = y * _rms(y) * gp_ref[...]

    a = a_ref[...]
    attn_n = a * _rms(a) * ga_ref[...]

    mixed = jnp.concatenate([attn_n, pool_n], axis=1).astype(BF16)
    h1 = x_ref[...] + jnp.dot(mixed, wo_ref[...], preferred_element_type=F32)

    h2 = (h1 * _rms(h1) * gf_ref[...]).astype(BF16)
    for c in range(D_FF // FF_CHUNK):
        cols = slice(c * FF_CHUNK, (c + 1) * FF_CHUNK)
        gate = jnp.dot(h2, wi_ref[:, cols], preferred_element_type=F32)
        up = jnp.dot(h2, wi_ref[:, D_FF + c * FF_CHUNK:D_FF + (c + 1) * FF_CHUNK],
                     preferred_element_type=F32)
        act_ref[:, cols] = (gate * jax.nn.sigmoid(gate) * up).astype(BF16)
    h3 = h1 + jnp.dot(act_ref[...], wd_ref[...], preferred_element_type=F32)
    o_ref[...] = h3 * _rms(h3) * gl_ref[...]


def _mix_ffn(x2d, attn, u, umeta, ga, gp, wp, wo, gf, wi, wd, gl, n_batch, tm):
    n_rows = x2d.shape[0]
    tiles = n_rows // tm
    tpb = tiles // n_batch
    hb = tm // POOL_HALO
    n_hblocks = n_rows // POOL_HALO
    kernel = functools.partial(_mix_ffn_kernel, tm=tm, tpb=tpb,
                               seq_total=N_META + n_rows // n_batch)

    def const(shape):
        zeros = (0,) * len(shape)
        return pl.BlockSpec(shape, lambda i: zeros, pipeline_mode=pl.Buffered(1))

    return pl.pallas_call(
        kernel,
        grid=(tiles,),
        in_specs=[
            pl.BlockSpec((tm, D_MODEL), lambda i: (i, 0)),
            pl.BlockSpec((tm, D_ATTN), lambda i: (i, 0)),
            pl.BlockSpec((tm, D_POOL), lambda i: (i, 0)),
            pl.BlockSpec((POOL_HALO, D_POOL), lambda i: (jnp.maximum(i * hb - 1, 0), 0)),
            pl.BlockSpec((POOL_HALO, D_POOL),
                         lambda i: (jnp.minimum((i + 1) * hb, n_hblocks - 1), 0)),
            const((N_META, D_POOL)),
            const((1, D_ATTN)),
            const((1, D_POOL)),
            const((2, MXU_DIM, MXU_DIM)),
            const((D_MODEL, D_MODEL)),
            const((1, D_MODEL)),
            const((D_MODEL, 2 * D_FF)),
            const((D_FF, D_MODEL)),
            const((1, D_MODEL)),
        ],
        out_specs=pl.BlockSpec((tm, D_MODEL), lambda i: (i, 0)),
        out_shape=jax.ShapeDtypeStruct((n_rows, D_MODEL), F32),
        scratch_shapes=[pltpu.VMEM((tm + 2 * POOL_HALO, D_POOL), F32),
                        pltpu.VMEM((tm, D_FF), BF16)],
        compiler_params=pltpu.CompilerParams(
            dimension_semantics=("parallel",), vmem_limit_bytes=VMEM_LIMIT),
        name="mix_ffn",
    )(x2d, attn, u, u, u, umeta, ga, gp, wp, wo, gf, wi, wd, gl)


def _rope_tables(n_real):
    rows = n_real // GRID_W
    r = jnp.arange(n_real, dtype=jnp.int32)
    row = (r // GRID_W - rows // 2).astype(F32)
    col = (r % GRID_W - GRID_W // 2).astype(F32)
    freqs = ROPE_THETA ** (-jnp.arange(ROPE_AXIS_PAIRS, dtype=F32) / ROPE_AXIS_PAIRS)
    ar = row[:, None] * freqs
    ac = col[:, None] * freqs
    cos = jnp.concatenate([jnp.cos(ar), jnp.cos(ar), jnp.cos(ac), jnp.cos(ac)], axis=1)
    sin = jnp.concatenate([-jnp.sin(ar), jnp.sin(ar), -jnp.sin(ac), jnp.sin(ac)], axis=1)
    return jnp.tile(cos, (1, 2)), jnp.tile(sin, (1, 2))


def kernel(x, meta_tokens, norm_mix, w_in, q_norm, k_norm, attn_out_norm, w_pool,
           pool_scale, w_out, norm_ffn, w_ffn_in, w_ffn_down, norm_final):
    n_batch, n_real, _ = x.shape
    assert n_real % TM_PROJ == 0 and n_real % TQ == 0 and n_real % TM_FFN == 0
    assert w_in.shape[0] == 1, "single layer"
    x2d = x.reshape(n_batch * n_real, D_MODEL)

    w_in_b = w_in[0].astype(BF16)
    g_mix = norm_mix[0].reshape(1, D_MODEL)
    head_gain = jnp.concatenate([jnp.tile(q_norm[0], N_HEADS) * (HEAD_DIM ** -0.5 * LOG2_E),
                                 jnp.tile(k_norm[0], N_KV_HEADS)]).reshape(1, D_QK)
    blk = jnp.arange(MXU_DIM) // HEAD_DIM
    e_heads = (blk[:, None] == blk[None, :]).astype(BF16)
    cos, sin = _rope_tables(n_real)

    wp = jnp.zeros((2, MXU_DIM, MXU_DIM), F32)
    for gi in range(len(POOL_WINDOWS)):
        r0 = (gi % 2) * POOL_GROUP
        wp = wp.at[gi // 2, r0:r0 + POOL_GROUP, r0:r0 + POOL_GROUP].set(w_pool[0, gi])
    wp = wp.astype(BF16)
    w_out_b = w_out[0].astype(BF16)
    wi = w_ffn_in[0].astype(BF16)
    wd = w_ffn_down[0].astype(BF16)

    qt, k2, vt, u = _inproj(x2d, g_mix, w_in_b, head_gain, e_heads, cos, sin, n_batch,
                            TM_PROJ, TQ)
    meta_pad = jnp.zeros((TQ, D_MODEL), F32).at[:N_META].set(meta_tokens)
    ident_cos = jnp.ones((TQ, LANES), F32)
    ident_sin = jnp.zeros((TQ, LANES), F32)
    _, kmeta2, vtmeta, umeta = _inproj(meta_pad, g_mix, w_in_b, head_gain, e_heads,
                                       ident_cos, ident_sin, 1, TQ, TQ)
    kmeta2 = kmeta2[:N_META]
    lane = jnp.arange(LANES)
    vtmeta = jnp.where(lane[None, None, :] < N_META, vtmeta[0, :, :, :LANES], 0).astype(BF16)
    umeta = umeta[:N_META]

    attn = _attention(qt, k2, vt, kmeta2, vtmeta, n_batch, TQ, TK)

    out = _mix_ffn(x2d, attn, u, umeta,
                   attn_out_norm[0].reshape(1, D_ATTN), pool_scale[0].reshape(1, D_POOL),
                   wp, w_out_b, norm_ffn[0].reshape(1, D_MODEL), wi, wd,
                   norm_final.reshape(1, D_MODEL), n_batch, TM_FFN)
    return out.reshape(n_batch, n_real, D_MODEL)
```

```python
import functools

import jax
import jax.numpy as jnp
from jax import lax
from jax.experimental import pallas as pl
from jax.experimental.pallas import tpu as pltpu

D_MODEL = 1024
N_META = 16
GRID_W = 64
N_HEADS = 8
N_KV_HEADS = 2
HEAD_DIM = 64
KV_GROUP = N_HEADS // N_KV_HEADS
D_ATTN = N_HEADS * HEAD_DIM
D_KV = N_KV_HEADS * HEAD_DIM
D_QK = D_ATTN + D_KV
D_POOL = 512
POOL_WINDOWS = (2, 4, 8, 16)
POOL_GROUP = 128
POOL_HALO = 8
EXT_SLACK = 16
D_IN = D_ATTN + 2 * D_KV + D_POOL
D_FF = 2816
ROPE_AXIS_PAIRS = HEAD_DIM // 4
ROPE_THETA = 10000.0
EPS = 1e-6

LANES = 128
MXU_DIM = 256
VT_ROWS = HEAD_DIM + 16
VMEM_LIMIT = 56 * 1024 * 1024

LOG2_E = 1.4426950408889634
NEG_BIG = -1e30

TM_PROJ = 512
TQ = 256
TK = 256
TM_FFN = 512
FF_CHUNK = 256

F32 = jnp.float32
BF16 = jnp.bfloat16


def _rms(x):
    return lax.rsqrt(jnp.mean(x * x, axis=-1, keepdims=True) + EPS)


def _inproj_kernel(x_ref, g_ref, w_ref, hg_ref, e_ref, cos_ref, sin_ref,
                   qt_ref, k_ref, vt_ref, u_ref):
    x = x_ref[...]
    h = (x * _rms(x) * g_ref[...]).astype(BF16)
    proj = jnp.dot(h, w_ref[...], preferred_element_type=F32)

    u_ref[...] = proj[:, D_QK + D_KV:]

    vt = proj[:, D_QK:D_QK + D_KV].T
    ones = jnp.ones((VT_ROWS - HEAD_DIM, vt.shape[1]), BF16)
    for g in range(N_KV_HEADS):
        vt_ref[0, g, 0:HEAD_DIM, :] = vt[g * HEAD_DIM:(g + 1) * HEAD_DIM].astype(BF16)
        vt_ref[0, g, HEAD_DIM:VT_ROWS, :] = ones

    qk = proj[:, :D_QK]
    sq = (qk * qk).astype(BF16)
    e = e_ref[...]
    ss = jnp.concatenate(
        [jnp.dot(sq[:, 0:256], e, preferred_element_type=F32),
         jnp.dot(sq[:, 256:512], e, preferred_element_type=F32),
         jnp.dot(sq[:, 512:640], e[:LANES, :LANES], preferred_element_type=F32)],
        axis=1)
    qn = qk * lax.rsqrt(ss * (1.0 / HEAD_DIM) + EPS) * hg_ref[...]

    cos = cos_ref[...]
    sin = sin_ref[...]
    lane = lax.broadcasted_iota(jnp.int32, cos.shape, 1)
    first_half = (lane & ROPE_AXIS_PAIRS) == 0
    roped = []
    for c in range(D_QK // LANES):
        xc = qn[:, c * LANES:(c + 1) * LANES]
        partner = jnp.where(first_half,
                            pltpu.roll(xc, LANES - ROPE_AXIS_PAIRS, 1),
                            pltpu.roll(xc, ROPE_AXIS_PAIRS, 1))
        roped.append(xc * cos + partner * sin)
    tq = qt_ref.shape[-1]
    for c in range(D_ATTN // LANES):
        qt = roped[c].T.astype(BF16)
        for t in range(qt_ref.shape[2]):
            qt_ref[0, c, t] = qt[:, t * tq:(t + 1) * tq]
    kc = roped[D_ATTN // LANES]
    k_ref[:, 0:LANES] = kc.astype(BF16)
    k_ref[:, LANES:2 * LANES] = pltpu.roll(kc, HEAD_DIM, 1).astype(BF16)


def _inproj(x2d, g, w, hg, e, cos, sin, n_batch, tm, tq):
    n_rows = x2d.shape[0]
    tiles = n_rows // tm
    tpb = tiles // n_batch
    n_pos = cos.shape[0] // tm
    qpt = tm // tq
    n_pairs = N_HEADS // 2
    const = lambda i: (0, 0)
    return pl.pallas_call(
        _inproj_kernel,
        grid=(tiles,),
        in_specs=[
            pl.BlockSpec((tm, D_MODEL), lambda i: (i, 0)),
            pl.BlockSpec((1, D_MODEL), const),
            pl.BlockSpec((D_MODEL, D_IN), const),
            pl.BlockSpec((1, D_QK), const),
            pl.BlockSpec((MXU_DIM, MXU_DIM), const),
            pl.BlockSpec((tm, LANES), lambda i: (i % n_pos, 0)),
            pl.BlockSpec((tm, LANES), lambda i: (i % n_pos, 0)),
        ],
        out_specs=[
            pl.BlockSpec((1, n_pairs, qpt, LANES, tq), lambda i: (i // tpb, 0, i % tpb, 0, 0)),
            pl.BlockSpec((tm, 2 * D_KV), lambda i: (i, 0)),
            pl.BlockSpec((1, N_KV_HEADS, VT_ROWS, tm), lambda i: (i // tpb, 0, 0, i % tpb)),
            pl.BlockSpec((tm, D_POOL), lambda i: (i, 0)),
        ],
        out_shape=[
            jax.ShapeDtypeStruct((n_batch, n_pairs, tpb * qpt, LANES, tq), BF16),
            jax.ShapeDtypeStruct((n_rows, 2 * D_KV), BF16),
            jax.ShapeDtypeStruct((n_batch, N_KV_HEADS, VT_ROWS, tpb * tm), BF16),
            jax.ShapeDtypeStruct((n_rows, D_POOL), F32),
        ],
        compiler_params=pltpu.CompilerParams(
            dimension_semantics=("parallel",), vmem_limit_bytes=VMEM_LIMIT),
        name="inproj",
    )(x2d, g, w, hg, e, cos, sin)


def _attn_kernel(q_ref, k_ref, vt_ref, kmeta_ref, vtmeta_ref, o_ref, kx_ref, vx_ref, s_ref,
                 *, tq, tk, seq):
    n_qt = seq // tq
    nq = 2 * tq
    n_chunks = seq // tk
    n_keys = seq + N_META
    bounds = [(c * tk, (c + 1) * tk) for c in range(n_chunks)]
    bounds[-1] = (bounds[-1][0], n_keys)

    kx_ref[0:seq, :] = k_ref[...]
    kx_ref[seq:n_keys, :] = kmeta_ref[...]
    vx_ref[:, 0:seq] = vt_ref[0, 0]
    vx_ref[:, seq:seq + LANES] = vtmeta_ref[0]

    def weights(j, pair):
        qt = q_ref[0, pair, j]
        z = jnp.zeros((HEAD_DIM, tq), BF16)
        return jnp.concatenate([jnp.concatenate([qt[0:HEAD_DIM], z], axis=0),
                                jnp.concatenate([qt[HEAD_DIM:], z], axis=0)], axis=1)

    def score_chunk(w, slot, c, mx):
        lo, hi = bounds[c]
        s = jnp.dot(kx_ref[lo:hi, :], w, preferred_element_type=F32)
        for head in range(2):
            s_ref[head, lo:hi, slot * tq:(slot + 1) * tq] = s[:, head * tq:(head + 1) * tq]
        return jnp.maximum(mx, jnp.max(s, axis=0, keepdims=True))

    def pv_chunk(slot, c, m, acc):
        lo, hi = bounds[c]
        s = jnp.concatenate([s_ref[head, lo:hi, slot * tq:(slot + 1) * tq] for head in range(2)],
                            axis=1)
        p = jnp.exp2(s - m).astype(BF16)
        if hi == n_keys:
            hi = seq + LANES
            p = jnp.concatenate([p, jnp.zeros((hi - n_keys, nq), BF16)], axis=0)
        return acc + jnp.dot(vx_ref[:, lo:hi], p, preferred_element_type=F32)

    def stage(w_next, slot_next, slot_cur, m_cur):
        mx = jnp.full((1, nq), NEG_BIG, F32)
        acc = jnp.zeros((VT_ROWS, nq), F32)
        for c in range(n_chunks):
            mx = score_chunk(w_next, slot_next, c, mx)
            acc = pv_chunk(slot_cur, c, m_cur, acc)
        return mx, acc

    def finish(acc, j, pair):
        o = acc[0:HEAD_DIM] * (1.0 / acc[HEAD_DIM:HEAD_DIM + 1])
        both = jnp.concatenate([o[:, :tq], o[:, tq:]], axis=0)
        o_ref[pl.ds(pl.multiple_of(j * tq, tq), tq), pair * LANES:(pair + 1) * LANES] = both.T

    w0 = weights(0, 0)
    m0 = jnp.full((1, nq), NEG_BIG, F32)
    for c in range(n_chunks):
        m0 = score_chunk(w0, 0, c, m0)

    def body(j, m_a):
        m_b, acc = stage(weights(j, 1), 1, 0, m_a)
        finish(acc, j, 0)
        m_a, acc = stage(weights(jnp.minimum(j + 1, n_qt - 1), 0), 0, 1, m_b)
        finish(acc, j, 1)
        return m_a

    lax.fori_loop(0, n_qt, body, m0)


def _attention(qt, k2, vt, kmeta2, vtmeta, n_batch, tq, tk):
    n_rows = k2.shape[0]
    seq = n_rows // n_batch
    n_qt = seq // tq
    gw = KV_GROUP * HEAD_DIM
    kernel = functools.partial(_attn_kernel, tq=tq, tk=tk, seq=seq)
    return pl.pallas_call(
        kernel,
        grid=(n_batch, N_KV_HEADS),
        in_specs=[
            pl.BlockSpec((1, KV_GROUP // 2, n_qt, LANES, tq), lambda b, g: (b, g, 0, 0, 0)),
            pl.BlockSpec((seq, LANES), lambda b, g: (b, g)),
            pl.BlockSpec((1, 1, VT_ROWS, seq), lambda b, g: (b, g, 0, 0)),
            pl.BlockSpec((N_META, LANES), lambda b, g: (0, g)),
            pl.BlockSpec((1, VT_ROWS, LANES), lambda b, g: (g, 0, 0)),
        ],
        out_specs=pl.BlockSpec((seq, gw), lambda b, g: (b, g)),
        out_shape=jax.ShapeDtypeStruct((n_rows, D_ATTN), F32),
        scratch_shapes=[pltpu.VMEM((seq + LANES, LANES), BF16),
                        pltpu.VMEM((VT_ROWS, seq + LANES), BF16),
                        pltpu.VMEM((2, seq + LANES, 2 * tq), F32)],
        compiler_params=pltpu.CompilerParams(
            dimension_semantics=("parallel", "parallel"), vmem_limit_bytes=VMEM_LIMIT),
        name="attention",
    )(qt, k2, vt, kmeta2, vtmeta)


def _mix_ffn_kernel(x_ref, a_ref, u_ref, uprev_ref, unext_ref, umeta_ref,
                    ga_ref, gp_ref, wp_ref, wo_ref, gf_ref, wi_ref, wd_ref, gl_ref,
                    o_ref, ext_ref, act_ref, *, tm, tpb, seq_total):
    j = pl.program_id(0) % tpb

    ext_ref[POOL_HALO:POOL_HALO + tm, :] = u_ref[...]
    ext_ref[0:POOL_HALO, :] = jnp.where(j == 0, umeta_ref[N_META - POOL_HALO:N_META, :],
                                        uprev_ref[...])
    ext_ref[POOL_HALO + tm:2 * POOL_HALO + tm, :] = jnp.where(j == tpb - 1, 0.0, unext_ref[...])
    ext_ref[2 * POOL_HALO + tm:, :] = jnp.zeros((EXT_SLACK, D_POOL), F32)

    t8 = j * tm + (tm - POOL_HALO + N_META) + lax.broadcasted_iota(
        jnp.int32, (POOL_HALO, POOL_GROUP), 0)
    diffs = []
    for gi, w in enumerate(POOL_WINDOWS):
        half = w // 2
        xg = ext_ref[:, gi * POOL_GROUP:(gi + 1) * POOL_GROUP]
        s, k, n = xg, 1, tm + 2 * POOL_HALO + EXT_SLACK
        while k < half:
            n -= POOL_HALO
            s = s[0:n] + s[k:k + n]
            k *= 2
        win = s[POOL_HALO - half:POOL_HALO - half + tm] + s[POOL_HALO:POOL_HALO + tm]
        u_self = xg[POOL_HALO:POOL_HALO + tm]
        cnt8 = jnp.minimum(t8 + half, seq_total) - (t8 - half)
        body = win[:tm - POOL_HALO] * (1.0 / w) - u_self[:tm - POOL_HALO]
        tail = win[tm - POOL_HALO:] * (1.0 / cnt8.astype(F32)) - u_self[tm - POOL_HALO:]
        diffs.append(jnp.concatenate([body, tail], axis=0))
    pooled = []
    for pr in range(2):
        d = jnp.concatenate(diffs[2 * pr:2 * pr + 2], axis=1).astype(BF16)
        pooled.append(jnp.dot(d, wp_ref[pr], preferred_element_type=F32))
    y = jnp.concatenate(pooled, axis=1)
    pool_n = y * _rms(y) * gp_ref[...]

    a = a_ref[...]
    attn_n = a * _rms(a) * ga_ref[...]

    mixed = jnp.concatenate([attn_n, pool_n], axis=1).astype(BF16)
    h1 = x_ref[...] + jnp.dot(mixed, wo_ref[...], preferred_element_type=F32)

    h2 = (h1 * _rms(h1) * gf_ref[...]).astype(BF16)
    for c in range(D_FF // FF_CHUNK):
        cols = slice(c * FF_CHUNK, (c + 1) * FF_CHUNK)
        gate = jnp.dot(h2, wi_ref[:, cols], preferred_element_type=F32)
        up = jnp.dot(h2, wi_ref[:, D_FF + c * FF_CHUNK:D_FF + (c + 1) * FF_CHUNK],
                     preferred_element_type=F32)
        act_ref[:, cols] = (gate * jax.nn.sigmoid(gate) * up).astype(BF16)
    h3 = h1 + jnp.dot(act_ref[...], wd_ref[...], preferred_element_type=F32)
    o_ref[...] = h3 * _rms(h3) * gl_ref[...]


def _mix_ffn(x2d, attn, u, umeta, ga, gp, wp, wo, gf, wi, wd, gl, n_batch, tm):
    n_rows = x2d.shape[0]
    tiles = n_rows // tm
    tpb = tiles // n_batch
    hb = tm // POOL_HALO
    n_hblocks = n_rows // POOL_HALO
    kernel = functools.partial(_mix_ffn_kernel, tm=tm, tpb=tpb,
                               seq_total=N_META + n_rows // n_batch)

    def const(shape):
        zeros = (0,) * len(shape)
        return pl.BlockSpec(shape, lambda i: zeros, pipeline_mode=pl.Buffered(1))

    return pl.pallas_call(
        kernel,
        grid=(tiles,),
        in_specs=[
            pl.BlockSpec((tm, D_MODEL), lambda i: (i, 0)),
            pl.BlockSpec((tm, D_ATTN), lambda i: (i, 0)),
            pl.BlockSpec((tm, D_POOL), lambda i: (i, 0)),
            pl.BlockSpec((POOL_HALO, D_POOL), lambda i: (jnp.maximum(i * hb - 1, 0), 0)),
            pl.BlockSpec((POOL_HALO, D_POOL),
                         lambda i: (jnp.minimum((i + 1) * hb, n_hblocks - 1), 0)),
            const((N_META, D_POOL)),
            const((1, D_ATTN)),
            const((1, D_POOL)),
            const((2, MXU_DIM, MXU_DIM)),
            const((D_MODEL, D_MODEL)),
            const((1, D_MODEL)),
            const((D_MODEL, 2 * D_FF)),
            const((D_FF, D_MODEL)),
            const((1, D_MODEL)),
        ],
        out_specs=pl.BlockSpec((tm, D_MODEL), lambda i: (i, 0)),
        out_shape=jax.ShapeDtypeStruct((n_rows, D_MODEL), F32),
        scratch_shapes=[pltpu.VMEM((tm + 2 * POOL_HALO + EXT_SLACK, D_POOL), F32),
                        pltpu.VMEM((tm, D_FF), BF16)],
        compiler_params=pltpu.CompilerParams(
            dimension_semantics=("parallel",), vmem_limit_bytes=VMEM_LIMIT),
        name="mix_ffn",
    )(x2d, attn, u, u, u, umeta, ga, gp, wp, wo, gf, wi, wd, gl)


def _rope_tables(n_real):
    rows = n_real // GRID_W
    r = jnp.arange(n_real, dtype=jnp.int32)
    row = (r // GRID_W - rows // 2).astype(F32)
    col = (r % GRID_W - GRID_W // 2).astype(F32)
    freqs = ROPE_THETA ** (-jnp.arange(ROPE_AXIS_PAIRS, dtype=F32) / ROPE_AXIS_PAIRS)
    ar = row[:, None] * freqs
    ac = col[:, None] * freqs
    cos = jnp.concatenate([jnp.cos(ar), jnp.cos(ar), jnp.cos(ac), jnp.cos(ac)], axis=1)
    sin = jnp.concatenate([-jnp.sin(ar), jnp.sin(ar), -jnp.sin(ac), jnp.sin(ac)], axis=1)
    return jnp.tile(cos, (1, 2)), jnp.tile(sin, (1, 2))


def kernel(x, meta_tokens, norm_mix, w_in, q_norm, k_norm, attn_out_norm, w_pool,
           pool_scale, w_out, norm_ffn, w_ffn_in, w_ffn_down, norm_final):
    n_batch, n_real, _ = x.shape
    assert n_real % TM_PROJ == 0 and n_real % TQ == 0 and n_real % TM_FFN == 0
    assert w_in.shape[0] == 1, "single layer"
    x2d = x.reshape(n_batch * n_real, D_MODEL)

    w_in_b = w_in[0].astype(BF16)
    g_mix = norm_mix[0].reshape(1, D_MODEL)
    head_gain = jnp.concatenate([jnp.tile(q_norm[0], N_HEADS) * (HEAD_DIM ** -0.5 * LOG2_E),
                                 jnp.tile(k_norm[0], N_KV_HEADS)]).reshape(1, D_QK)
    blk = jnp.arange(MXU_DIM) // HEAD_DIM
    e_heads = (blk[:, None] == blk[None, :]).astype(BF16)
    cos, sin = _rope_tables(n_real)

    wp = jnp.zeros((2, MXU_DIM, MXU_DIM), F32)
    for gi in range(len(POOL_WINDOWS)):
        r0 = (gi % 2) * POOL_GROUP
        wp = wp.at[gi // 2, r0:r0 + POOL_GROUP, r0:r0 + POOL_GROUP].set(w_pool[0, gi])
    wp = wp.astype(BF16)
    w_out_b = w_out[0].astype(BF16)
    wi = w_ffn_in[0].astype(BF16)
    wd = w_ffn_down[0].astype(BF16)

    qt, k2, vt, u = _inproj(x2d, g_mix, w_in_b, head_gain, e_heads, cos, sin, n_batch,
                            TM_PROJ, TQ)
    meta_pad = jnp.zeros((TQ, D_MODEL), F32).at[:N_META].set(meta_tokens)
    ident_cos = jnp.ones((TQ, LANES), F32)
    ident_sin = jnp.zeros((TQ, LANES), F32)
    _, kmeta2, vtmeta, umeta = _inproj(meta_pad, g_mix, w_in_b, head_gain, e_heads,
                                       ident_cos, ident_sin, 1, TQ, TQ)
    kmeta2 = kmeta2[:N_META]
    lane = jnp.arange(LANES)
    vtmeta = jnp.where(lane[None, None, :] < N_META, vtmeta[0, :, :, :LANES], 0).astype(BF16)
    umeta = umeta[:N_META]

    attn = _attention(qt, k2, vt, kmeta2, vtmeta, n_batch, TQ, TK)

    out = _mix_ffn(x2d, attn, u, umeta,
                   attn_out_norm[0].reshape(1, D_ATTN), pool_scale[0].reshape(1, D_POOL),
                   wp, w_out_b, norm_ffn[0].reshape(1, D_MODEL), wi, wd,
                   norm_final.reshape(1, D_MODEL), n_batch, TM_FFN)
    return out.reshape(n_batch, n_real, D_MODEL)
```

```python
import functools

import jax
import jax.numpy as jnp
from jax import lax
from jax.experimental import pallas as pl
from jax.experimental.pallas import tpu as pltpu

D_MODEL = 1024
N_META = 16
GRID_W = 64
N_HEADS = 8
N_KV_HEADS = 2
HEAD_DIM = 64
KV_GROUP = N_HEADS // N_KV_HEADS
D_ATTN = N_HEADS * HEAD_DIM
D_KV = N_KV_HEADS * HEAD_DIM
D_QK = D_ATTN + D_KV
D_POOL = 512
POOL_WINDOWS = (2, 4, 8, 16)
POOL_GROUP = 128
POOL_HALO = 8
EXT_SLACK = 16
D_IN = D_ATTN + 2 * D_KV + D_POOL
D_FF = 2816
ROPE_AXIS_PAIRS = HEAD_DIM // 4
ROPE_THETA = 10000.0
EPS = 1e-6

LANES = 128
MXU_DIM = 256
VT_ROWS = HEAD_DIM + 16
VMEM_LIMIT = 56 * 1024 * 1024

LOG2_E = 1.4426950408889634
NEG_BIG = -1e30

TM_PROJ = 1024
TQ = 256
TK = 256
TM_FFN = 512
FF_CHUNK = 256

F32 = jnp.float32
BF16 = jnp.bfloat16


def _rms(x):
    return lax.rsqrt(jnp.mean(x * x, axis=-1, keepdims=True) + EPS)


def _inproj_kernel(x_ref, w_ref, hg_ref, e_ref, cos_ref, sin_ref,
                   qt_ref, k_ref, vt_ref, u_ref):
    x = x_ref[...]
    h = x.astype(BF16)
    r = _rms(x)
    e = e_ref[...]
    cos = cos_ref[...]
    sin = sin_ref[...]
    lane = lax.broadcasted_iota(jnp.int32, cos.shape, 1)
    first_half = (lane & ROPE_AXIS_PAIRS) == 0
    tq = qt_ref.shape[-1]

    def project(c0, width):
        return jnp.dot(h, w_ref[:, c0:c0 + width], preferred_element_type=F32) * r

    def head_norm_rope(p, c0):
        width = p.shape[1]
        ss = jnp.dot((p * p).astype(BF16), e[:width, :width], preferred_element_type=F32)
        pn = p * lax.rsqrt(ss * (1.0 / HEAD_DIM) + EPS) * hg_ref[:, c0:c0 + width]
        out = []
        for c in range(width // LANES):
            xc = pn[:, c * LANES:(c + 1) * LANES]
            partner = jnp.where(first_half,
                                pltpu.roll(xc, LANES - ROPE_AXIS_PAIRS, 1),
                                pltpu.roll(xc, ROPE_AXIS_PAIRS, 1))
            out.append(xc * cos + partner * sin)
        return out

    def q_epilogue(p, blk):
        roped = head_norm_rope(p, blk * MXU_DIM)
        for c, rc in enumerate(roped):
            qt = rc.T.astype(BF16)
            for t in range(qt_ref.shape[2]):
                qt_ref[0, blk * (MXU_DIM // LANES) + c, t] = qt[:, t * tq:(t + 1) * tq]

    def kv_epilogue(kv):
        kc, = head_norm_rope(kv[:, :D_KV], D_ATTN)
        k_ref[:, 0:LANES] = kc.astype(BF16)
        k_ref[:, LANES:2 * LANES] = pltpu.roll(kc, HEAD_DIM, 1).astype(BF16)
        vt = kv[:, D_KV:].T
        ones = jnp.ones((VT_ROWS - HEAD_DIM, vt.shape[1]), BF16)
        for g in range(N_KV_HEADS):
            vt_ref[0, g, 0:HEAD_DIM, :] = vt[g * HEAD_DIM:(g + 1) * HEAD_DIM].astype(BF16)
            vt_ref[0, g, HEAD_DIM:VT_ROWS, :] = ones

    p_q0 = project(0, MXU_DIM)
    p_q1 = project(MXU_DIM, MXU_DIM)
    q_epilogue(p_q0, 0)
    p_kv = project(D_ATTN, 2 * D_KV)
    q_epilogue(p_q1, 1)
    u_ref[:, 0:MXU_DIM] = project(D_QK + D_KV, MXU_DIM)
    kv_epilogue(p_kv)
    u_ref[:, MXU_DIM:] = project(D_QK + D_KV + MXU_DIM, MXU_DIM)


def _inproj(x2d, w, hg, e, cos, sin, n_batch, tm, tq):
    n_rows = x2d.shape[0]
    tiles = n_rows // tm
    tpb = tiles // n_batch
    n_pos = cos.shape[0] // tm
    qpt = tm // tq
    n_pairs = N_HEADS // 2
    const = lambda i: (0, 0)
    return pl.pallas_call(
        _inproj_kernel,
        grid=(tiles,),
        in_specs=[
            pl.BlockSpec((tm, D_MODEL), lambda i: (i, 0)),
            pl.BlockSpec((D_MODEL, D_IN), const),
            pl.BlockSpec((1, D_QK), const),
            pl.BlockSpec((MXU_DIM, MXU_DIM), const),
            pl.BlockSpec((tm, LANES), lambda i: (i % n_pos, 0)),
            pl.BlockSpec((tm, LANES), lambda i: (i % n_pos, 0)),
        ],
        out_specs=[
            pl.BlockSpec((1, n_pairs, qpt, LANES, tq), lambda i: (i // tpb, 0, i % tpb, 0, 0)),
            pl.BlockSpec((tm, 2 * D_KV), lambda i: (i, 0)),
            pl.BlockSpec((1, N_KV_HEADS, VT_ROWS, tm), lambda i: (i // tpb, 0, 0, i % tpb)),
            pl.BlockSpec((tm, D_POOL), lambda i: (i, 0)),
        ],
        out_shape=[
            jax.ShapeDtypeStruct((n_batch, n_pairs, tpb * qpt, LANES, tq), BF16),
            jax.ShapeDtypeStruct((n_rows, 2 * D_KV), BF16),
            jax.ShapeDtypeStruct((n_batch, N_KV_HEADS, VT_ROWS, tpb * tm), BF16),
            jax.ShapeDtypeStruct((n_rows, D_POOL), F32),
        ],
        compiler_params=pltpu.CompilerParams(
            dimension_semantics=("parallel",), vmem_limit_bytes=VMEM_LIMIT),
        name="inproj",
    )(x2d, w, hg, e, cos, sin)


def _attn_kernel(q_ref, k_ref, vt_ref, kmeta_ref, vtmeta_ref, o_ref, kx_ref, vx_ref, s_ref,
                 *, tq, tk, seq):
    n_qt = seq // tq
    nq = 2 * tq
    n_chunks = seq // tk
    n_keys = seq + N_META
    bounds = [(c * tk, (c + 1) * tk) for c in range(n_chunks)]
    bounds[-1] = (bounds[-1][0], n_keys)

    kx_ref[0:seq, :] = k_ref[...]
    kx_ref[seq:n_keys, :] = kmeta_ref[...]
    vx_ref[:, 0:seq] = vt_ref[0, 0]
    vx_ref[:, seq:seq + LANES] = vtmeta_ref[0]

    def weights(j, pair):
        qt = q_ref[0, pair, j]
        z = jnp.zeros((HEAD_DIM, tq), BF16)
        return jnp.concatenate([jnp.concatenate([qt[0:HEAD_DIM], z], axis=0),
                                jnp.concatenate([qt[HEAD_DIM:], z], axis=0)], axis=1)

    def score_chunk(w, slot, c, mx):
        lo, hi = bounds[c]
        s = jnp.dot(kx_ref[lo:hi, :], w, preferred_element_type=F32)
        for head in range(2):
            s_ref[head, lo:hi, slot * tq:(slot + 1) * tq] = s[:, head * tq:(head + 1) * tq]
        return jnp.maximum(mx, jnp.max(s, axis=0, keepdims=True))

    def pv_chunk(slot, c, m, acc):
        lo, hi = bounds[c]
        s = jnp.concatenate([s_ref[head, lo:hi, slot * tq:(slot + 1) * tq] for head in range(2)],
                            axis=1)
        p = jnp.exp2(s - m).astype(BF16)
        if hi == n_keys:
            hi = seq + LANES
            p = jnp.concatenate([p, jnp.zeros((hi - n_keys, nq), BF16)], axis=0)
        return acc + jnp.dot(vx_ref[:, lo:hi], p, preferred_element_type=F32)

    def stage(w_next, slot_next, slot_cur, m_cur):
        mx = jnp.full((1, nq), NEG_BIG, F32)
        acc = jnp.zeros((VT_ROWS, nq), F32)
        for c in range(n_chunks):
            mx = score_chunk(w_next, slot_next, c, mx)
            acc = pv_chunk(slot_cur, c, m_cur, acc)
        return mx, acc

    def finish(acc, j, pair):
        o = acc[0:HEAD_DIM] * (1.0 / acc[HEAD_DIM:HEAD_DIM + 1])
        both = jnp.concatenate([o[:, :tq], o[:, tq:]], axis=0)
        o_ref[pl.ds(pl.multiple_of(j * tq, tq), tq), pair * LANES:(pair + 1) * LANES] = both.T

    w0 = weights(0, 0)
    m0 = jnp.full((1, nq), NEG_BIG, F32)
    for c in range(n_chunks):
        m0 = score_chunk(w0, 0, c, m0)

    def body(j, m_a):
        m_b, acc = stage(weights(j, 1), 1, 0, m_a)
        finish(acc, j, 0)
        m_a, acc = stage(weights(jnp.minimum(j + 1, n_qt - 1), 0), 0, 1, m_b)
        finish(acc, j, 1)
        return m_a

    lax.fori_loop(0, n_qt, body, m0)


def _attention(qt, k2, vt, kmeta2, vtmeta, n_batch, tq, tk):
    n_rows = k2.shape[0]
    seq = n_rows // n_batch
    n_qt = seq // tq
    gw = KV_GROUP * HEAD_DIM
    kernel = functools.partial(_attn_kernel, tq=tq, tk=tk, seq=seq)
    return pl.pallas_call(
        kernel,
        grid=(n_batch, N_KV_HEADS),
        in_specs=[
            pl.BlockSpec((1, KV_GROUP // 2, n_qt, LANES, tq), lambda b, g: (b, g, 0, 0, 0)),
            pl.BlockSpec((seq, LANES), lambda b, g: (b, g)),
            pl.BlockSpec((1, 1, VT_ROWS, seq), lambda b, g: (b, g, 0, 0)),
            pl.BlockSpec((N_META, LANES), lambda b, g: (0, g)),
            pl.BlockSpec((1, VT_ROWS, LANES), lambda b, g: (g, 0, 0)),
        ],
        out_specs=pl.BlockSpec((seq, gw), lambda b, g: (b, g)),
        out_shape=jax.ShapeDtypeStruct((n_rows, D_ATTN), F32),
        scratch_shapes=[pltpu.VMEM((seq + LANES, LANES), BF16),
                        pltpu.VMEM((VT_ROWS, seq + LANES), BF16),
                        pltpu.VMEM((2, seq + LANES, 2 * tq), F32)],
        compiler_params=pltpu.CompilerParams(
            dimension_semantics=("parallel", "parallel"), vmem_limit_bytes=VMEM_LIMIT),
        name="attention",
    )(qt, k2, vt, kmeta2, vtmeta)


def _mix_ffn_kernel(x_ref, a_ref, u_ref, uprev_ref, unext_ref, umeta_ref,
                    wp_ref, wo_ref, wi_ref, wd_ref, gl_ref,
                    o_ref, ext_ref, act_ref, *, tm, tpb, seq_total):
    j = pl.program_id(0) % tpb

    a = a_ref[...]
    proj_a = jnp.dot(a.astype(BF16), wo_ref[0:D_ATTN, :], preferred_element_type=F32) * _rms(a)

    ext_ref[POOL_HALO:POOL_HALO + tm, :] = u_ref[...]
    ext_ref[0:POOL_HALO, :] = jnp.where(j == 0, umeta_ref[N_META - POOL_HALO:N_META, :],
                                        uprev_ref[...])
    ext_ref[POOL_HALO + tm:2 * POOL_HALO + tm, :] = jnp.where(j == tpb - 1, 0.0, unext_ref[...])
    ext_ref[2 * POOL_HALO + tm:, :] = jnp.zeros((EXT_SLACK, D_POOL), F32)

    t8 = j * tm + (tm - POOL_HALO + N_META) + lax.broadcasted_iota(
        jnp.int32, (POOL_HALO, POOL_GROUP), 0)
    diffs = []
    for gi, w in enumerate(POOL_WINDOWS):
        half = w // 2
        xg = ext_ref[:, gi * POOL_GROUP:(gi + 1) * POOL_GROUP]
        s, k, n = xg, 1, tm + 2 * POOL_HALO + EXT_SLACK
        while k < half:
            n -= POOL_HALO
            s = s[0:n] + s[k:k + n]
            k *= 2
        win = s[POOL_HALO - half:POOL_HALO - half + tm] + s[POOL_HALO:POOL_HALO + tm]
        u_self = xg[POOL_HALO:POOL_HALO + tm]
        cnt8 = jnp.minimum(t8 + half, seq_total) - (t8 - half)
        body = win[:tm - POOL_HALO] * (1.0 / w) - u_self[:tm - POOL_HALO]
        tail = win[tm - POOL_HALO:] * (1.0 / cnt8.astype(F32)) - u_self[tm - POOL_HALO:]
        diffs.append(jnp.concatenate([body, tail], axis=0))
    pooled = []
    for pr in range(2):
        d = jnp.concatenate(diffs[2 * pr:2 * pr + 2], axis=1).astype(BF16)
        pooled.append(jnp.dot(d, wp_ref[pr], preferred_element_type=F32))
    y = jnp.concatenate(pooled, axis=1)
    proj_p = jnp.dot(y.astype(BF16), wo_ref[D_ATTN:, :], preferred_element_type=F32) * _rms(y)
    h1 = x_ref[...] + proj_a + proj_p

    h1b = h1.astype(BF16)
    r1 = _rms(h1)
    for c in range(D_FF // FF_CHUNK):
        cols = slice(c * FF_CHUNK, (c + 1) * FF_CHUNK)
        gate = jnp.dot(h1b, wi_ref[:, cols], preferred_element_type=F32) * r1
        up = jnp.dot(h1b, wi_ref[:, D_FF + c * FF_CHUNK:D_FF + (c + 1) * FF_CHUNK],
                     preferred_element_type=F32) * r1
        act_ref[:, cols] = (gate * jax.nn.sigmoid(gate) * up).astype(BF16)
    h3 = h1 + jnp.dot(act_ref[...], wd_ref[...], preferred_element_type=F32)
    o_ref[...] = h3 * _rms(h3) * gl_ref[...]


def _mix_ffn(x2d, attn, u, umeta, wp, wo, wi, wd, gl, n_batch, tm):
    n_rows = x2d.shape[0]
    tiles = n_rows // tm
    tpb = tiles // n_batch
    hb = tm // POOL_HALO
    n_hblocks = n_rows // POOL_HALO
    kernel = functools.partial(_mix_ffn_kernel, tm=tm, tpb=tpb,
                               seq_total=N_META + n_rows // n_batch)

    def const(shape):
        zeros = (0,) * len(shape)
        return pl.BlockSpec(shape, lambda i: zeros, pipeline_mode=pl.Buffered(1))

    return pl.pallas_call(
        kernel,
        grid=(tiles,),
        in_specs=[
            pl.BlockSpec((tm, D_MODEL), lambda i: (i, 0)),
            pl.BlockSpec((tm, D_ATTN), lambda i: (i, 0)),
            pl.BlockSpec((tm, D_POOL), lambda i: (i, 0)),
            pl.BlockSpec((POOL_HALO, D_POOL), lambda i: (jnp.maximum(i * hb - 1, 0), 0)),
            pl.BlockSpec((POOL_HALO, D_POOL),
                         lambda i: (jnp.minimum((i + 1) * hb, n_hblocks - 1), 0)),
            const((N_META, D_POOL)),
            const((2, MXU_DIM, MXU_DIM)),
            const((D_MODEL, D_MODEL)),
            const((D_MODEL, 2 * D_FF)),
            const((D_FF, D_MODEL)),
            const((1, D_MODEL)),
        ],
        out_specs=pl.BlockSpec((tm, D_MODEL), lambda i: (i, 0)),
        out_shape=jax.ShapeDtypeStruct((n_rows, D_MODEL), F32),
        scratch_shapes=[pltpu.VMEM((tm + 2 * POOL_HALO + EXT_SLACK, D_POOL), F32),
                        pltpu.VMEM((tm, D_FF), BF16)],
        compiler_params=pltpu.CompilerParams(
            dimension_semantics=("parallel",), vmem_limit_bytes=VMEM_LIMIT),
        name="mix_ffn",
    )(x2d, attn, u, u, u, umeta, wp, wo, wi, wd, gl)


def _rope_tables(n_real):
    rows = n_real // GRID_W
    r = jnp.arange(n_real, dtype=jnp.int32)
    row = (r // GRID_W - rows // 2).astype(F32)
    col = (r % GRID_W - GRID_W // 2).astype(F32)
    freqs = ROPE_THETA ** (-jnp.arange(ROPE_AXIS_PAIRS, dtype=F32) / ROPE_AXIS_PAIRS)
    ar = row[:, None] * freqs
    ac = col[:, None] * freqs
    cos = jnp.concatenate([jnp.cos(ar), jnp.cos(ar), jnp.cos(ac), jnp.cos(ac)], axis=1)
    sin = jnp.concatenate([-jnp.sin(ar), jnp.sin(ar), -jnp.sin(ac), jnp.sin(ac)], axis=1)
    return jnp.tile(cos, (1, 2)), jnp.tile(sin, (1, 2))


def kernel(x, meta_tokens, norm_mix, w_in, q_norm, k_norm, attn_out_norm, w_pool,
           pool_scale, w_out, norm_ffn, w_ffn_in, w_ffn_down, norm_final):
    n_batch, n_real, _ = x.shape
    assert n_real % TM_PROJ == 0 and n_real % TQ == 0 and n_real % TM_FFN == 0
    assert w_in.shape[0] == 1, "single layer"
    x2d = x.reshape(n_batch * n_real, D_MODEL)

    w_in_b = (norm_mix[0][:, None] * w_in[0]).astype(BF16)
    head_gain = jnp.concatenate([jnp.tile(q_norm[0], N_HEADS) * (HEAD_DIM ** -0.5 * LOG2_E),
                                 jnp.tile(k_norm[0], N_KV_HEADS)]).reshape(1, D_QK)
    blk = jnp.arange(MXU_DIM) // HEAD_DIM
    e_heads = (blk[:, None] == blk[None, :]).astype(BF16)
    cos, sin = _rope_tables(n_real)

    wp = jnp.zeros((2, MXU_DIM, MXU_DIM), F32)
    for gi in range(len(POOL_WINDOWS)):
        r0 = (gi % 2) * POOL_GROUP
        wp = wp.at[gi // 2, r0:r0 + POOL_GROUP, r0:r0 + POOL_GROUP].set(w_pool[0, gi])
    wp = wp.astype(BF16)
    mix_gain = jnp.concatenate([attn_out_norm[0], pool_scale[0]])
    w_out_b = (mix_gain[:, None] * w_out[0]).astype(BF16)
    wi = (norm_ffn[0][:, None] * w_ffn_in[0]).astype(BF16)
    wd = w_ffn_down[0].astype(BF16)

    qt, k2, vt, u = _inproj(x2d, w_in_b, head_gain, e_heads, cos, sin, n_batch, TM_PROJ, TQ)
    meta_pad = jnp.zeros((TQ, D_MODEL), F32).at[:N_META].set(meta_tokens)
    ident_cos = jnp.ones((TQ, LANES), F32)
    ident_sin = jnp.zeros((TQ, LANES), F32)
    _, kmeta2, vtmeta, umeta = _inproj(meta_pad, w_in_b, head_gain, e_heads,
                                       ident_cos, ident_sin, 1, TQ, TQ)
    kmeta2 = kmeta2[:N_META]
    lane = jnp.arange(LANES)
    vtmeta = jnp.where(lane[None, None, :] < N_META, vtmeta[0, :, :, :LANES], 0).astype(BF16)
    umeta = umeta[:N_META]

    attn = _attention(qt, k2, vt, kmeta2, vtmeta, n_batch, TQ, TK)

    out = _mix_ffn(x2d, attn, u, umeta, wp, w_out_b, wi, wd,
                   norm_final.reshape(1, D_MODEL), n_batch, TM_FFN)
    return out.reshape(n_batch, n_real, D_MODEL)
```

```python
import functools

import jax
import jax.numpy as jnp
from jax import lax
from jax.experimental import pallas as pl
from jax.experimental.pallas import tpu as pltpu

D_MODEL = 1024
N_META = 16
GRID_W = 64
N_HEADS = 8
N_KV_HEADS = 2
HEAD_DIM = 64
KV_GROUP = N_HEADS // N_KV_HEADS
D_ATTN = N_HEADS * HEAD_DIM
D_KV = N_KV_HEADS * HEAD_DIM
D_QK = D_ATTN + D_KV
D_POOL = 512
POOL_WINDOWS = (2, 4, 8, 16)
POOL_GROUP = 128
POOL_HALO = 8
EXT_SLACK = 16
D_IN = D_ATTN + 2 * D_KV + D_POOL
D_FF = 2816
ROPE_AXIS_PAIRS = HEAD_DIM // 4
ROPE_THETA = 10000.0
EPS = 1e-6

LANES = 128
MXU_DIM = 256
VT_ROWS = HEAD_DIM + 16
VMEM_LIMIT = 56 * 1024 * 1024

LOG2_E = 1.4426950408889634
NEG_BIG = -1e30

TM_PROJ = 1024
TQ = 256
TK = 256
TILES_PER_ITER = 4
FINISH_AT_CHUNK = 1
TM_FFN = 512
FF_CHUNK = 256

F32 = jnp.float32
BF16 = jnp.bfloat16


def _rms(x):
    return lax.rsqrt(jnp.mean(x * x, axis=-1, keepdims=True) + EPS)


def _inproj_kernel(x_ref, w_ref, hg_ref, e_ref, cos_ref, sin_ref,
                   qt_ref, k_ref, vt_ref, u_ref):
    x = x_ref[...]
    h = x.astype(BF16)
    r = _rms(x)
    e = e_ref[...]
    cos = cos_ref[...]
    sin = sin_ref[...]
    lane = lax.broadcasted_iota(jnp.int32, cos.shape, 1)
    first_half = (lane & ROPE_AXIS_PAIRS) == 0
    tq = qt_ref.shape[-1]

    def project(c0, width):
        return jnp.dot(h, w_ref[:, c0:c0 + width], preferred_element_type=F32) * r

    def head_norm_rope(p, c0):
        width = p.shape[1]
        ss = jnp.dot((p * p).astype(BF16), e[:width, :width], preferred_element_type=F32)
        pn = p * lax.rsqrt(ss * (1.0 / HEAD_DIM) + EPS) * hg_ref[:, c0:c0 + width]
        out = []
        for c in range(width // LANES):
            xc = pn[:, c * LANES:(c + 1) * LANES]
            partner = jnp.where(first_half,
                                pltpu.roll(xc, LANES - ROPE_AXIS_PAIRS, 1),
                                pltpu.roll(xc, ROPE_AXIS_PAIRS, 1))
            out.append(xc * cos + partner * sin)
        return out

    def q_epilogue(p, blk):
        roped = head_norm_rope(p, blk * MXU_DIM)
        for c, rc in enumerate(roped):
            qt = rc.T.astype(BF16)
            for t in range(qt_ref.shape[2]):
                qt_ref[0, blk * (MXU_DIM // LANES) + c, t] = qt[:, t * tq:(t + 1) * tq]

    def kv_epilogue(kv):
        kc, = head_norm_rope(kv[:, :D_KV], D_ATTN)
        k_ref[:, 0:LANES] = kc.astype(BF16)
        k_ref[:, LANES:2 * LANES] = pltpu.roll(kc, HEAD_DIM, 1).astype(BF16)
        vt = kv[:, D_KV:].T
        ones = jnp.ones((VT_ROWS - HEAD_DIM, vt.shape[1]), BF16)
        for g in range(N_KV_HEADS):
            vt_ref[0, g, 0:HEAD_DIM, :] = vt[g * HEAD_DIM:(g + 1) * HEAD_DIM].astype(BF16)
            vt_ref[0, g, HEAD_DIM:VT_ROWS, :] = ones

    p_q0 = project(0, MXU_DIM)
    p_q1 = project(MXU_DIM, MXU_DIM)
    q_epilogue(p_q0, 0)
    p_kv = project(D_ATTN, 2 * D_KV)
    q_epilogue(p_q1, 1)
    u_ref[:, 0:MXU_DIM] = project(D_QK + D_KV, MXU_DIM)
    kv_epilogue(p_kv)
    u_ref[:, MXU_DIM:] = project(D_QK + D_KV + MXU_DIM, MXU_DIM)


def _inproj(x2d, w, hg, e, cos, sin, n_batch, tm, tq):
    n_rows = x2d.shape[0]
    tiles = n_rows // tm
    tpb = tiles // n_batch
    n_pos = cos.shape[0] // tm
    qpt = tm // tq
    n_pairs = N_HEADS // 2
    const = lambda i: (0, 0)
    return pl.pallas_call(
        _inproj_kernel,
        grid=(tiles,),
        in_specs=[
            pl.BlockSpec((tm, D_MODEL), lambda i: (i, 0)),
            pl.BlockSpec((D_MODEL, D_IN), const),
            pl.BlockSpec((1, D_QK), const),
            pl.BlockSpec((MXU_DIM, MXU_DIM), const),
            pl.BlockSpec((tm, LANES), lambda i: (i % n_pos, 0)),
            pl.BlockSpec((tm, LANES), lambda i: (i % n_pos, 0)),
        ],
        out_specs=[
            pl.BlockSpec((1, n_pairs, qpt, LANES, tq), lambda i: (i // tpb, 0, i % tpb, 0, 0)),
            pl.BlockSpec((tm, 2 * D_KV), lambda i: (i, 0)),
            pl.BlockSpec((1, N_KV_HEADS, VT_ROWS, tm), lambda i: (i // tpb, 0, 0, i % tpb)),
            pl.BlockSpec((tm, D_POOL), lambda i: (i, 0)),
        ],
        out_shape=[
            jax.ShapeDtypeStruct((n_batch, n_pairs, tpb * qpt, LANES, tq), BF16),
            jax.ShapeDtypeStruct((n_rows, 2 * D_KV), BF16),
            jax.ShapeDtypeStruct((n_batch, N_KV_HEADS, VT_ROWS, tpb * tm), BF16),
            jax.ShapeDtypeStruct((n_rows, D_POOL), F32),
        ],
        compiler_params=pltpu.CompilerParams(
            dimension_semantics=("parallel",), vmem_limit_bytes=VMEM_LIMIT),
        name="inproj",
    )(x2d, w, hg, e, cos, sin)


def _attn_kernel(q_ref, k_ref, vt_ref, kmeta_ref, vtmeta_ref, o_ref, kx_ref, vx_ref, s_ref,
                 *, tq, tk, seq):
    n_qt = seq // tq
    nq = 2 * tq
    n_chunks = seq // tk
    n_keys = seq + N_META
    bounds = [(c * tk, (c + 1) * tk) for c in range(n_chunks)]
    bounds[-1] = (bounds[-1][0], n_keys)

    kx_ref[0:seq, :] = k_ref[...]
    kx_ref[seq:n_keys, :] = kmeta_ref[...]
    vx_ref[:, 0:seq] = vt_ref[0, 0]
    vx_ref[:, seq:seq + LANES] = vtmeta_ref[0]

    def weights(j, pair):
        qt = q_ref[0, pair, j]
        z = jnp.zeros((HEAD_DIM, tq), BF16)
        return jnp.concatenate([jnp.concatenate([qt[0:HEAD_DIM], z], axis=0),
                                jnp.concatenate([qt[HEAD_DIM:], z], axis=0)], axis=1)

    def score_chunk(w, slot, c, mx):
        lo, hi = bounds[c]
        s = jnp.dot(kx_ref[lo:hi, :], w, preferred_element_type=F32)
        for head in range(2):
            s_ref[head, lo:hi, slot * tq:(slot + 1) * tq] = s[:, head * tq:(head + 1) * tq]
        return jnp.maximum(mx, jnp.max(s, axis=0, keepdims=True))

    def pv_chunk(slot, c, m, acc):
        lo, hi = bounds[c]
        s = jnp.concatenate([s_ref[head, lo:hi, slot * tq:(slot + 1) * tq] for head in range(2)],
                            axis=1)
        p = jnp.exp2(s - m).astype(BF16)
        if hi == n_keys:
            hi = seq + LANES
            p = jnp.concatenate([p, jnp.zeros((hi - n_keys, nq), BF16)], axis=0)
        return acc + jnp.dot(vx_ref[:, lo:hi], p, preferred_element_type=F32)

    def stage(w_next, slot_next, slot_cur, m_cur, finish_prev):
        mx = jnp.full((1, nq), NEG_BIG, F32)
        acc = jnp.zeros((VT_ROWS, nq), F32)
        for c in range(n_chunks):
            mx = score_chunk(w_next, slot_next, c, mx)
            acc = pv_chunk(slot_cur, c, m_cur, acc)
            if c == FINISH_AT_CHUNK and finish_prev is not None:
                finish_prev()
        return mx, acc

    def finish(acc, j, pair):
        o = acc[0:HEAD_DIM] * (1.0 / acc[HEAD_DIM:HEAD_DIM + 1])
        both = jnp.concatenate([o[:, :tq], o[:, tq:]], axis=0)
        o_ref[pl.ds(pl.multiple_of(j * tq, tq), tq), pair * LANES:(pair + 1) * LANES] = both.T

    w0 = weights(0, 0)
    m0 = jnp.full((1, nq), NEG_BIG, F32)
    for c in range(n_chunks):
        m0 = score_chunk(w0, 0, c, m0)

    def body(jj, m_cur):
        finish_prev = None
        for t in range(TILES_PER_ITER):
            j = jj * TILES_PER_ITER + t
            for pair in range(2):
                nxt = (j, 1) if pair == 0 else (jnp.minimum(j + 1, n_qt - 1), 0)
                m_next, acc = stage(weights(*nxt), 1 - pair, pair, m_cur, finish_prev)
                finish_prev = functools.partial(finish, acc, j, pair)
                m_cur = m_next
        finish_prev()
        return m_cur

    lax.fori_loop(0, n_qt // TILES_PER_ITER, body, m0)


def _attention(qt, k2, vt, kmeta2, vtmeta, n_batch, tq, tk):
    n_rows = k2.shape[0]
    seq = n_rows // n_batch
    n_qt = seq // tq
    gw = KV_GROUP * HEAD_DIM
    kernel = functools.partial(_attn_kernel, tq=tq, tk=tk, seq=seq)
    return pl.pallas_call(
        kernel,
        grid=(n_batch, N_KV_HEADS),
        in_specs=[
            pl.BlockSpec((1, KV_GROUP // 2, n_qt, LANES, tq), lambda b, g: (b, g, 0, 0, 0)),
            pl.BlockSpec((seq, LANES), lambda b, g: (b, g)),
            pl.BlockSpec((1, 1, VT_ROWS, seq), lambda b, g: (b, g, 0, 0)),
            pl.BlockSpec((N_META, LANES), lambda b, g: (0, g)),
            pl.BlockSpec((1, VT_ROWS, LANES), lambda b, g: (g, 0, 0)),
        ],
        out_specs=pl.BlockSpec((seq, gw), lambda b, g: (b, g)),
        out_shape=jax.ShapeDtypeStruct((n_rows, D_ATTN), F32),
        scratch_shapes=[pltpu.VMEM((seq + LANES, LANES), BF16),
                        pltpu.VMEM((VT_ROWS, seq + LANES), BF16),
                        pltpu.VMEM((2, seq + LANES, 2 * tq), F32)],
        compiler_params=pltpu.CompilerParams(
            dimension_semantics=("parallel", "parallel"), vmem_limit_bytes=VMEM_LIMIT),
        name="attention",
    )(qt, k2, vt, kmeta2, vtmeta)


def _mix_ffn_kernel(x_ref, a_ref, u_ref, uprev_ref, unext_ref, umeta_ref,
                    wp_ref, wo_ref, wi_ref, wd_ref, gl_ref,
                    o_ref, ext_ref, act_ref, *, tm, tpb, seq_total):
    j = pl.program_id(0) % tpb

    a = a_ref[...]
    proj_a = jnp.dot(a.astype(BF16), wo_ref[0:D_ATTN, :], preferred_element_type=F32) * _rms(a)

    ext_ref[POOL_HALO:POOL_HALO + tm, :] = u_ref[...]
    ext_ref[0:POOL_HALO, :] = jnp.where(j == 0, umeta_ref[N_META - POOL_HALO:N_META, :],
                                        uprev_ref[...])
    ext_ref[POOL_HALO + tm:2 * POOL_HALO + tm, :] = jnp.where(j == tpb - 1, 0.0, unext_ref[...])
    ext_ref[2 * POOL_HALO + tm:, :] = jnp.zeros((EXT_SLACK, D_POOL), F32)

    t8 = j * tm + (tm - POOL_HALO + N_META) + lax.broadcasted_iota(
        jnp.int32, (POOL_HALO, POOL_GROUP), 0)
    diffs = []
    for gi, w in enumerate(POOL_WINDOWS):
        half = w // 2
        xg = ext_ref[:, gi * POOL_GROUP:(gi + 1) * POOL_GROUP]
        s, k, n = xg, 1, tm + 2 * POOL_HALO + EXT_SLACK
        while k < half:
            n -= POOL_HALO
            s = s[0:n] + s[k:k + n]
            k *= 2
        win = s[POOL_HALO - half:POOL_HALO - half + tm] + s[POOL_HALO:POOL_HALO + tm]
        u_self = xg[POOL_HALO:POOL_HALO + tm]
        cnt8 = jnp.minimum(t8 + half, seq_total) - (t8 - half)
        body = win[:tm - POOL_HALO] * (1.0 / w) - u_self[:tm - POOL_HALO]
        tail = win[tm - POOL_HALO:] * (1.0 / cnt8.astype(F32)) - u_self[tm - POOL_HALO:]
        diffs.append(jnp.concatenate([body, tail], axis=0))
    pooled = []
    for pr in range(2):
        d = jnp.concatenate(diffs[2 * pr:2 * pr + 2], axis=1).astype(BF16)
        pooled.append(jnp.dot(d, wp_ref[pr], preferred_element_type=F32))
    y = jnp.concatenate(pooled, axis=1)
    proj_p = jnp.dot(y.astype(BF16), wo_ref[D_ATTN:, :], preferred_element_type=F32) * _rms(y)
    h1 = x_ref[...] + proj_a + proj_p

    h1b = h1.astype(BF16)
    r1 = _rms(h1)
    for c in range(D_FF // FF_CHUNK):
        cols = slice(c * FF_CHUNK, (c + 1) * FF_CHUNK)
        gate = jnp.dot(h1b, wi_ref[:, cols], preferred_element_type=F32) * r1
        up = jnp.dot(h1b, wi_ref[:, D_FF + c * FF_CHUNK:D_FF + (c + 1) * FF_CHUNK],
                     preferred_element_type=F32) * r1
        act_ref[:, cols] = (gate * jax.nn.sigmoid(gate) * up).astype(BF16)
    h3 = h1 + jnp.dot(act_ref[...], wd_ref[...], preferred_element_type=F32)
    o_ref[...] = h3 * _rms(h3) * gl_ref[...]


def _mix_ffn(x2d, attn, u, umeta, wp, wo, wi, wd, gl, n_batch, tm):
    n_rows = x2d.shape[0]
    tiles = n_rows // tm
    tpb = tiles // n_batch
    hb = tm // POOL_HALO
    n_hblocks = n_rows // POOL_HALO
    kernel = functools.partial(_mix_ffn_kernel, tm=tm, tpb=tpb,
                               seq_total=N_META + n_rows // n_batch)

    def const(shape):
        zeros = (0,) * len(shape)
        return pl.BlockSpec(shape, lambda i: zeros, pipeline_mode=pl.Buffered(1))

    return pl.pallas_call(
        kernel,
        grid=(tiles,),
        in_specs=[
            pl.BlockSpec((tm, D_MODEL), lambda i: (i, 0)),
            pl.BlockSpec((tm, D_ATTN), lambda i: (i, 0)),
            pl.BlockSpec((tm, D_POOL), lambda i: (i, 0)),
            pl.BlockSpec((POOL_HALO, D_POOL), lambda i: (jnp.maximum(i * hb - 1, 0), 0)),
            pl.BlockSpec((POOL_HALO, D_POOL),
                         lambda i: (jnp.minimum((i + 1) * hb, n_hblocks - 1), 0)),
            const((N_META, D_POOL)),
            const((2, MXU_DIM, MXU_DIM)),
            const((D_MODEL, D_MODEL)),
            const((D_MODEL, 2 * D_FF)),
            const((D_FF, D_MODEL)),
            const((1, D_MODEL)),
        ],
        out_specs=pl.BlockSpec((tm, D_MODEL), lambda i: (i, 0)),
        out_shape=jax.ShapeDtypeStruct((n_rows, D_MODEL), F32),
        scratch_shapes=[pltpu.VMEM((tm + 2 * POOL_HALO + EXT_SLACK, D_POOL), F32),
                        pltpu.VMEM((tm, D_FF), BF16)],
        compiler_params=pltpu.CompilerParams(
            dimension_semantics=("parallel",), vmem_limit_bytes=VMEM_LIMIT),
        name="mix_ffn",
    )(x2d, attn, u, u, u, umeta, wp, wo, wi, wd, gl)


def _rope_tables(n_real):
    rows = n_real // GRID_W
    r = jnp.arange(n_real, dtype=jnp.int32)
    row = (r // GRID_W - rows // 2).astype(F32)
    col = (r % GRID_W - GRID_W // 2).astype(F32)
    freqs = ROPE_THETA ** (-jnp.arange(ROPE_AXIS_PAIRS, dtype=F32) / ROPE_AXIS_PAIRS)
    ar = row[:, None] * freqs
    ac = col[:, None] * freqs
    cos = jnp.concatenate([jnp.cos(ar), jnp.cos(ar), jnp.cos(ac), jnp.cos(ac)], axis=1)
    sin = jnp.concatenate([-jnp.sin(ar), jnp.sin(ar), -jnp.sin(ac), jnp.sin(ac)], axis=1)
    return jnp.tile(cos, (1, 2)), jnp.tile(sin, (1, 2))


def kernel(x, meta_tokens, norm_mix, w_in, q_norm, k_norm, attn_out_norm, w_pool,
           pool_scale, w_out, norm_ffn, w_ffn_in, w_ffn_down, norm_final):
    n_batch, n_real, _ = x.shape
    assert n_real % TM_PROJ == 0 and n_real % TQ == 0 and n_real % TM_FFN == 0
    assert w_in.shape[0] == 1, "single layer"
    x2d = x.reshape(n_batch * n_real, D_MODEL)

    w_in_b = (norm_mix[0][:, None] * w_in[0]).astype(BF16)
    head_gain = jnp.concatenate([jnp.tile(q_norm[0], N_HEADS) * (HEAD_DIM ** -0.5 * LOG2_E),
                                 jnp.tile(k_norm[0], N_KV_HEADS)]).reshape(1, D_QK)
    blk = jnp.arange(MXU_DIM) // HEAD_DIM
    e_heads = (blk[:, None] == blk[None, :]).astype(BF16)
    cos, sin = _rope_tables(n_real)

    wp = jnp.zeros((2, MXU_DIM, MXU_DIM), F32)
    for gi in range(len(POOL_WINDOWS)):
        r0 = (gi % 2) * POOL_GROUP
        wp = wp.at[gi // 2, r0:r0 + POOL_GROUP, r0:r0 + POOL_GROUP].set(w_pool[0, gi])
    wp = wp.astype(BF16)
    mix_gain = jnp.concatenate([attn_out_norm[0], pool_scale[0]])
    w_out_b = (mix_gain[:, None] * w_out[0]).astype(BF16)
    wi = (norm_ffn[0][:, None] * w_ffn_in[0]).astype(BF16)
    wd = w_ffn_down[0].astype(BF16)

    qt, k2, vt, u = _inproj(x2d, w_in_b, head_gain, e_heads, cos, sin, n_batch, TM_PROJ, TQ)
    meta_pad = jnp.zeros((TQ, D_MODEL), F32).at[:N_META].set(meta_tokens)
    ident_cos = jnp.ones((TQ, LANES), F32)
    ident_sin = jnp.zeros((TQ, LANES), F32)
    _, kmeta2, vtmeta, umeta = _inproj(meta_pad, w_in_b, head_gain, e_heads,
                                       ident_cos, ident_sin, 1, TQ, TQ)
    kmeta2 = kmeta2[:N_META]
    lane = jnp.arange(LANES)
    vtmeta = jnp.where(lane[None, None, :] < N_META, vtmeta[0, :, :, :LANES], 0).astype(BF16)
    umeta = umeta[:N_META]

    attn = _attention(qt, k2, vt, kmeta2, vtmeta, n_batch, TQ, TK)

    out = _mix_ffn(x2d, attn, u, umeta, wp, w_out_b, wi, wd,
                   norm_final.reshape(1, D_MODEL), n_batch, TM_FFN)
    return out.reshape(n_batch, n_real, D_MODEL)
```

```python
import functools

import jax
import jax.numpy as jnp
import numpy as np
from jax import lax
from jax.experimental import pallas as pl
from jax.experimental.pallas import tpu as pltpu

D_MODEL = 1024
N_META = 16
GRID_W = 64
N_HEADS = 8
N_KV_HEADS = 2
HEAD_DIM = 64
KV_GROUP = N_HEADS // N_KV_HEADS
D_ATTN = N_HEADS * HEAD_DIM
D_KV = N_KV_HEADS * HEAD_DIM
D_QK = D_ATTN + D_KV
D_POOL = 512
POOL_WINDOWS = (2, 4, 8, 16)
POOL_GROUP = 128
POOL_HALO = 8
EXT_SLACK = 16
D_IN = D_ATTN + 2 * D_KV + D_POOL
D_FF = 2816
ROPE_AXIS_PAIRS = HEAD_DIM // 4
ROPE_THETA = 10000.0
EPS = 1e-6

LANES = 128
MXU_DIM = 256
VT_ROWS = HEAD_DIM + 16
VMEM_LIMIT = 56 * 1024 * 1024

LOG2_E = 1.4426950408889634
NEG_BIG = -1e30

TM_PROJ = 1024
TQ = 256
TK = 256
TILES_PER_ITER = 8
FINISH_AT_CHUNK = 1
TM_FFN = 512
FF_CHUNK = 256

F32 = jnp.float32
BF16 = jnp.bfloat16


def _rms(x):
    return lax.rsqrt(jnp.mean(x * x, axis=-1, keepdims=True) + EPS)


def _inproj_kernel(x_ref, w_ref, hg_ref, e_ref, cos_ref, sin_ref,
                   qt_ref, k_ref, vt_ref, u_ref):
    x = x_ref[...]
    h = x.astype(BF16)
    r = _rms(x)
    e = e_ref[...]
    cos = cos_ref[...]
    sin = sin_ref[...]
    lane = lax.broadcasted_iota(jnp.int32, cos.shape, 1)
    first_half = (lane & ROPE_AXIS_PAIRS) == 0
    tq = qt_ref.shape[-1]

    def project(c0, width):
        return jnp.dot(h, w_ref[:, c0:c0 + width], preferred_element_type=F32) * r

    def head_norm_rope(p, c0):
        width = p.shape[1]
        ss = jnp.dot((p * p).astype(BF16), e[:width, :width], preferred_element_type=F32)
        pn = p * lax.rsqrt(ss * (1.0 / HEAD_DIM) + EPS) * hg_ref[:, c0:c0 + width]
        out = []
        for c in range(width // LANES):
            xc = pn[:, c * LANES:(c + 1) * LANES]
            partner = jnp.where(first_half,
                                pltpu.roll(xc, LANES - ROPE_AXIS_PAIRS, 1),
                                pltpu.roll(xc, ROPE_AXIS_PAIRS, 1))
            out.append(xc * cos + partner * sin)
        return out

    def q_epilogue(p, blk):
        roped = head_norm_rope(p, blk * MXU_DIM)
        for c, rc in enumerate(roped):
            qt = rc.T.astype(BF16)
            for t in range(qt_ref.shape[2]):
                qt_ref[0, blk * (MXU_DIM // LANES) + c, t] = qt[:, t * tq:(t + 1) * tq]

    def kv_epilogue(kv):
        kc, = head_norm_rope(kv[:, :D_KV], D_ATTN)
        k_ref[:, 0:LANES] = kc.astype(BF16)
        k_ref[:, LANES:2 * LANES] = pltpu.roll(kc, HEAD_DIM, 1).astype(BF16)
        vt = kv[:, D_KV:].T
        ones = jnp.ones((VT_ROWS - HEAD_DIM, vt.shape[1]), BF16)
        for g in range(N_KV_HEADS):
            vt_ref[0, g, 0:HEAD_DIM, :] = vt[g * HEAD_DIM:(g + 1) * HEAD_DIM].astype(BF16)
            vt_ref[0, g, HEAD_DIM:VT_ROWS, :] = ones

    p_q0 = project(0, MXU_DIM)
    p_q1 = project(MXU_DIM, MXU_DIM)
    q_epilogue(p_q0, 0)
    p_kv = project(D_ATTN, 2 * D_KV)
    q_epilogue(p_q1, 1)
    u_ref[:, 0:MXU_DIM] = project(D_QK + D_KV, MXU_DIM)
    kv_epilogue(p_kv)
    u_ref[:, MXU_DIM:] = project(D_QK + D_KV + MXU_DIM, MXU_DIM)


def _inproj(x2d, w, hg, e, cos, sin, n_batch, tm, tq):
    n_rows = x2d.shape[0]
    tiles = n_rows // tm
    tpb = tiles // n_batch
    n_pos = cos.shape[0] // tm
    qpt = tm // tq
    n_pairs = N_HEADS // 2
    const = lambda i: (0, 0)
    return pl.pallas_call(
        _inproj_kernel,
        grid=(tiles,),
        in_specs=[
            pl.BlockSpec((tm, D_MODEL), lambda i: (i, 0)),
            pl.BlockSpec((D_MODEL, D_IN), const),
            pl.BlockSpec((1, D_QK), const),
            pl.BlockSpec((MXU_DIM, MXU_DIM), const),
            pl.BlockSpec((tm, LANES), lambda i: (i % n_pos, 0)),
            pl.BlockSpec((tm, LANES), lambda i: (i % n_pos, 0)),
        ],
        out_specs=[
            pl.BlockSpec((1, n_pairs, qpt, LANES, tq), lambda i: (i // tpb, 0, i % tpb, 0, 0)),
            pl.BlockSpec((tm, 2 * D_KV), lambda i: (i, 0)),
            pl.BlockSpec((1, N_KV_HEADS, VT_ROWS, tm), lambda i: (i // tpb, 0, 0, i % tpb)),
            pl.BlockSpec((tm, D_POOL), lambda i: (i, 0)),
        ],
        out_shape=[
            jax.ShapeDtypeStruct((n_batch, n_pairs, tpb * qpt, LANES, tq), BF16),
            jax.ShapeDtypeStruct((n_rows, 2 * D_KV), BF16),
            jax.ShapeDtypeStruct((n_batch, N_KV_HEADS, VT_ROWS, tpb * tm), BF16),
            jax.ShapeDtypeStruct((n_rows, D_POOL), F32),
        ],
        compiler_params=pltpu.CompilerParams(
            dimension_semantics=("parallel",), vmem_limit_bytes=VMEM_LIMIT),
        name="inproj",
    )(x2d, w, hg, e, cos, sin)


def _attn_kernel(q_ref, k_ref, vt_ref, kmeta_ref, vtmeta_ref,
                 wo_ref, gmix_ref, wi_ref, gffn_ref, wd_ref,
                 o_ref, wo_out, wi_out, wd_out, kx_ref, vx_ref, s_ref,
                 *, tq, tk, seq):
    wo_out[...] = (wo_ref[...] * gmix_ref[...]).astype(BF16)
    wi_out[...] = (wi_ref[...] * gffn_ref[...]).astype(BF16)
    wd_out[...] = wd_ref[...].astype(BF16)

    n_qt = seq // tq
    nq = 2 * tq
    n_chunks = seq // tk
    n_keys = seq + N_META
    bounds = [(c * tk, (c + 1) * tk) for c in range(n_chunks)]
    bounds[-1] = (bounds[-1][0], n_keys)

    kx_ref[0:seq, :] = k_ref[...]
    kx_ref[seq:n_keys, :] = kmeta_ref[...]
    vx_ref[:, 0:seq] = vt_ref[0, 0]
    vx_ref[:, seq:seq + LANES] = vtmeta_ref[0]

    def weights(j, pair):
        qt = q_ref[0, pair, j]
        z = jnp.zeros((HEAD_DIM, tq), BF16)
        return jnp.concatenate([jnp.concatenate([qt[0:HEAD_DIM], z], axis=0),
                                jnp.concatenate([qt[HEAD_DIM:], z], axis=0)], axis=1)

    def score_chunk(w, slot, c, mx):
        lo, hi = bounds[c]
        s = jnp.dot(kx_ref[lo:hi, :], w, preferred_element_type=F32)
        for head in range(2):
            s_ref[head, lo:hi, slot * tq:(slot + 1) * tq] = s[:, head * tq:(head + 1) * tq]
        return jnp.maximum(mx, jnp.max(s, axis=0, keepdims=True))

    def pv_chunk(slot, c, m, acc):
        lo, hi = bounds[c]
        s = jnp.concatenate([s_ref[head, lo:hi, slot * tq:(slot + 1) * tq] for head in range(2)],
                            axis=1)
        p = jnp.exp2(s - m).astype(BF16)
        if hi == n_keys:
            hi = seq + LANES
            p = jnp.concatenate([p, jnp.zeros((hi - n_keys, nq), BF16)], axis=0)
        return acc + jnp.dot(vx_ref[:, lo:hi], p, preferred_element_type=F32)

    def stage(w_next, slot_next, slot_cur, m_cur, finish_prev):
        mx = jnp.full((1, nq), NEG_BIG, F32)
        acc = jnp.zeros((VT_ROWS, nq), F32)
        for c in range(n_chunks):
            mx = score_chunk(w_next, slot_next, c, mx)
            acc = pv_chunk(slot_cur, c, m_cur, acc)
            if c == FINISH_AT_CHUNK and finish_prev is not None:
                finish_prev()
        return mx, acc

    def finish(acc, j, pair):
        o = acc[0:HEAD_DIM] * (1.0 / acc[HEAD_DIM:HEAD_DIM + 1])
        both = jnp.concatenate([o[:, :tq], o[:, tq:]], axis=0)
        o_ref[pl.ds(pl.multiple_of(j * tq, tq), tq), pair * LANES:(pair + 1) * LANES] = both.T

    w0 = weights(0, 0)
    m0 = jnp.full((1, nq), NEG_BIG, F32)
    for c in range(n_chunks):
        m0 = score_chunk(w0, 0, c, m0)

    def body(jj, m_cur):
        finish_prev = None
        for t in range(TILES_PER_ITER):
            j = jj * TILES_PER_ITER + t
            for pair in range(2):
                nxt = (j, 1) if pair == 0 else (jnp.minimum(j + 1, n_qt - 1), 0)
                m_next, acc = stage(weights(*nxt), 1 - pair, pair, m_cur, finish_prev)
                finish_prev = functools.partial(finish, acc, j, pair)
                m_cur = m_next
        finish_prev()
        return m_cur

    lax.fori_loop(0, n_qt // TILES_PER_ITER, body, m0)


def _attention(qt, k2, vt, kmeta2, vtmeta, w_out, g_mix, w_ffn_in, g_ffn, w_ffn_down,
               n_batch, tq, tk):
    n_rows = k2.shape[0]
    seq = n_rows // n_batch
    n_qt = seq // tq
    gw = KV_GROUP * HEAD_DIM
    n_steps = n_batch * N_KV_HEADS
    kernel = functools.partial(_attn_kernel, tq=tq, tk=tk, seq=seq)

    def slab(w):
        rows = w.shape[0] // n_steps
        assert rows * n_steps == w.shape[0] and rows % 16 == 0, w.shape
        return pl.BlockSpec((rows, w.shape[1]), lambda b, g: (b * N_KV_HEADS + g, 0))

    weights_in = (w_out, g_mix, w_ffn_in, g_ffn, w_ffn_down)
    weights_out = (w_out, w_ffn_in, w_ffn_down)
    return pl.pallas_call(
        kernel,
        grid=(n_batch, N_KV_HEADS),
        in_specs=[
            pl.BlockSpec((1, KV_GROUP // 2, n_qt, LANES, tq), lambda b, g: (b, g, 0, 0, 0)),
            pl.BlockSpec((seq, LANES), lambda b, g: (b, g)),
            pl.BlockSpec((1, 1, VT_ROWS, seq), lambda b, g: (b, g, 0, 0)),
            pl.BlockSpec((N_META, LANES), lambda b, g: (0, g)),
            pl.BlockSpec((1, VT_ROWS, LANES), lambda b, g: (g, 0, 0)),
        ] + [slab(w) for w in weights_in],
        out_specs=[pl.BlockSpec((seq, gw), lambda b, g: (b, g))] + [slab(w) for w in weights_out],
        out_shape=[jax.ShapeDtypeStruct((n_rows, D_ATTN), F32)]
        + [jax.ShapeDtypeStruct(w.shape, BF16) for w in weights_out],
        scratch_shapes=[pltpu.VMEM((seq + LANES, LANES), BF16),
                        pltpu.VMEM((VT_ROWS, seq + LANES), BF16),
                        pltpu.VMEM((2, seq + LANES, 2 * tq), F32)],
        compiler_params=pltpu.CompilerParams(
            dimension_semantics=("parallel", "parallel"), vmem_limit_bytes=VMEM_LIMIT),
        name="attention",
    )(qt, k2, vt, kmeta2, vtmeta, *weights_in)


def _mix_ffn_kernel(x_ref, a_ref, u_ref, uprev_ref, unext_ref, umeta_ref,
                    wp_ref, wo_ref, wi_ref, wd_ref, gl_ref,
                    o_ref, ext_ref, act_ref, *, tm, tpb, seq_total):
    j = pl.program_id(0) % tpb

    a = a_ref[...]
    proj_a = jnp.dot(a.astype(BF16), wo_ref[0:D_ATTN, :], preferred_element_type=F32) * _rms(a)

    ext_ref[POOL_HALO:POOL_HALO + tm, :] = u_ref[...]
    ext_ref[0:POOL_HALO, :] = jnp.where(j == 0, umeta_ref[N_META - POOL_HALO:N_META, :],
                                        uprev_ref[...])
    ext_ref[POOL_HALO + tm:2 * POOL_HALO + tm, :] = jnp.where(j == tpb - 1, 0.0, unext_ref[...])
    ext_ref[2 * POOL_HALO + tm:, :] = jnp.zeros((EXT_SLACK, D_POOL), F32)

    t8 = j * tm + (tm - POOL_HALO + N_META) + lax.broadcasted_iota(
        jnp.int32, (POOL_HALO, POOL_GROUP), 0)
    diffs = []
    for gi, w in enumerate(POOL_WINDOWS):
        half = w // 2
        xg = ext_ref[:, gi * POOL_GROUP:(gi + 1) * POOL_GROUP]
        s, k, n = xg, 1, tm + 2 * POOL_HALO + EXT_SLACK
        while k < half:
            n -= POOL_HALO
            s = s[0:n] + s[k:k + n]
            k *= 2
        win = s[POOL_HALO - half:POOL_HALO - half + tm] + s[POOL_HALO:POOL_HALO + tm]
        u_self = xg[POOL_HALO:POOL_HALO + tm]
        cnt8 = jnp.minimum(t8 + half, seq_total) - (t8 - half)
        body = win[:tm - POOL_HALO] * (1.0 / w) - u_self[:tm - POOL_HALO]
        tail = win[tm - POOL_HALO:] * (1.0 / cnt8.astype(F32)) - u_self[tm - POOL_HALO:]
        diffs.append(jnp.concatenate([body, tail], axis=0))
    pooled = []
    for pr in range(2):
        d = jnp.concatenate(diffs[2 * pr:2 * pr + 2], axis=1).astype(BF16)
        pooled.append(jnp.dot(d, wp_ref[pr], preferred_element_type=F32))
    y = jnp.concatenate(pooled, axis=1)
    proj_p = jnp.dot(y.astype(BF16), wo_ref[D_ATTN:, :], preferred_element_type=F32) * _rms(y)
    h1 = x_ref[...] + proj_a + proj_p

    h1b = h1.astype(BF16)
    r1 = _rms(h1)
    for c in range(D_FF // FF_CHUNK):
        cols = slice(c * FF_CHUNK, (c + 1) * FF_CHUNK)
        gate = jnp.dot(h1b, wi_ref[:, cols], preferred_element_type=F32) * r1
        up = jnp.dot(h1b, wi_ref[:, D_FF + c * FF_CHUNK:D_FF + (c + 1) * FF_CHUNK],
                     preferred_element_type=F32) * r1
        act_ref[:, cols] = (gate * jax.nn.sigmoid(gate) * up).astype(BF16)
    h3 = h1 + jnp.dot(act_ref[...], wd_ref[...], preferred_element_type=F32)
    o_ref[...] = h3 * _rms(h3) * gl_ref[...]


def _mix_ffn(x2d, attn, u, umeta, wp, wo, wi, wd, gl, n_batch, tm):
    n_rows = x2d.shape[0]
    tiles = n_rows // tm
    tpb = tiles // n_batch
    hb = tm // POOL_HALO
    n_hblocks = n_rows // POOL_HALO
    kernel = functools.partial(_mix_ffn_kernel, tm=tm, tpb=tpb,
                               seq_total=N_META + n_rows // n_batch)

    def const(shape):
        zeros = (0,) * len(shape)
        return pl.BlockSpec(shape, lambda i: zeros, pipeline_mode=pl.Buffered(1))

    return pl.pallas_call(
        kernel,
        grid=(tiles,),
        in_specs=[
            pl.BlockSpec((tm, D_MODEL), lambda i: (i, 0)),
            pl.BlockSpec((tm, D_ATTN), lambda i: (i, 0)),
            pl.BlockSpec((tm, D_POOL), lambda i: (i, 0)),
            pl.BlockSpec((POOL_HALO, D_POOL), lambda i: (jnp.maximum(i * hb - 1, 0), 0)),
            pl.BlockSpec((POOL_HALO, D_POOL),
                         lambda i: (jnp.minimum((i + 1) * hb, n_hblocks - 1), 0)),
            const((N_META, D_POOL)),
            const((2, MXU_DIM, MXU_DIM)),
            const((D_MODEL, D_MODEL)),
            const((D_MODEL, 2 * D_FF)),
            const((D_FF, D_MODEL)),
            const((1, D_MODEL)),
        ],
        out_specs=pl.BlockSpec((tm, D_MODEL), lambda i: (i, 0)),
        out_shape=jax.ShapeDtypeStruct((n_rows, D_MODEL), F32),
        scratch_shapes=[pltpu.VMEM((tm + 2 * POOL_HALO + EXT_SLACK, D_POOL), F32),
                        pltpu.VMEM((tm, D_FF), BF16)],
        compiler_params=pltpu.CompilerParams(
            dimension_semantics=("parallel",), vmem_limit_bytes=VMEM_LIMIT),
        name="mix_ffn",
    )(x2d, attn, u, u, u, umeta, wp, wo, wi, wd, gl)


def _rope_tables(n_real):
    rows = n_real // GRID_W
    r = np.arange(n_real)
    row = (r // GRID_W - rows // 2).astype(np.float64)
    col = (r % GRID_W - GRID_W // 2).astype(np.float64)
    freqs = ROPE_THETA ** (-np.arange(ROPE_AXIS_PAIRS, dtype=np.float64) / ROPE_AXIS_PAIRS)
    ar = row[:, None] * freqs
    ac = col[:, None] * freqs
    cos = np.concatenate([np.cos(ar), np.cos(ar), np.cos(ac), np.cos(ac)], axis=1)
    sin = np.concatenate([-np.sin(ar), np.sin(ar), -np.sin(ac), np.sin(ac)], axis=1)
    return (jnp.asarray(np.tile(cos, (1, 2)), dtype=F32),
            jnp.asarray(np.tile(sin, (1, 2)), dtype=F32))


def kernel(x, meta_tokens, norm_mix, w_in, q_norm, k_norm, attn_out_norm, w_pool,
           pool_scale, w_out, norm_ffn, w_ffn_in, w_ffn_down, norm_final):
    n_batch, n_real, _ = x.shape
    assert n_real % TM_PROJ == 0 and n_real % TQ == 0 and n_real % TM_FFN == 0
    assert w_in.shape[0] == 1, "single layer"
    x2d = x.reshape(n_batch * n_real, D_MODEL)

    w_in_b = (norm_mix[0][:, None] * w_in[0]).astype(BF16)
    head_gain = jnp.concatenate([jnp.tile(q_norm[0], N_HEADS) * (HEAD_DIM ** -0.5 * LOG2_E),
                                 jnp.tile(k_norm[0], N_KV_HEADS)]).reshape(1, D_QK)
    blk = np.arange(MXU_DIM) // HEAD_DIM
    e_heads = jnp.asarray(blk[:, None] == blk[None, :], dtype=BF16)
    cos, sin = _rope_tables(n_real)

    wp = jnp.zeros((2, MXU_DIM, MXU_DIM), F32)
    for gi in range(len(POOL_WINDOWS)):
        r0 = (gi % 2) * POOL_GROUP
        wp = wp.at[gi // 2, r0:r0 + POOL_GROUP, r0:r0 + POOL_GROUP].set(w_pool[0, gi])
    wp = wp.astype(BF16)
    g_mix = jnp.concatenate([attn_out_norm[0], pool_scale[0]]).reshape(D_MODEL, 1)
    g_ffn = norm_ffn[0].reshape(D_MODEL, 1)

    qt, k2, vt, u = _inproj(x2d, w_in_b, head_gain, e_heads, cos, sin, n_batch, TM_PROJ, TQ)
    meta_pad = jnp.zeros((TQ, D_MODEL), F32).at[:N_META].set(meta_tokens)
    ident_cos = jnp.asarray(np.ones((TQ, LANES), np.float32))
    ident_sin = jnp.asarray(np.zeros((TQ, LANES), np.float32))
    _, kmeta2, vtmeta, umeta = _inproj(meta_pad, w_in_b, head_gain, e_heads,
                                       ident_cos, ident_sin, 1, TQ, TQ)
    kmeta2 = kmeta2[:N_META]
    lane = jnp.arange(LANES)
    vtmeta = jnp.where(lane[None, None, :] < N_META, vtmeta[0, :, :, :LANES], 0).astype(BF16)
    umeta = umeta[:N_META]

    attn, w_out_b, wi, wd = _attention(qt, k2, vt, kmeta2, vtmeta,
                                       w_out[0], g_mix, w_ffn_in[0], g_ffn, w_ffn_down[0],
                                       n_batch, TQ, TK)

    out = _mix_ffn(x2d, attn, u, umeta, wp, w_out_b, wi, wd,
                   norm_final.reshape(1, D_MODEL), n_batch, TM_FFN)
    return out.reshape(n_batch, n_real, D_MODEL)
```

```python
import functools

import jax
import jax.numpy as jnp
import numpy as np
from jax import lax
from jax.experimental import pallas as pl
from jax.experimental.pallas import tpu as pltpu

D_MODEL = 1024
N_META = 16
GRID_W = 64
N_HEADS = 8
N_KV_HEADS = 2
HEAD_DIM = 64
KV_GROUP = N_HEADS // N_KV_HEADS
D_ATTN = N_HEADS * HEAD_DIM
D_KV = N_KV_HEADS * HEAD_DIM
D_QK = D_ATTN + D_KV
D_POOL = 512
POOL_WINDOWS = (2, 4, 8, 16)
POOL_GROUP = 128
POOL_HALO = 8
EXT_SLACK = 16
D_IN = D_ATTN + 2 * D_KV + D_POOL
D_FF = 2816
ROPE_AXIS_PAIRS = HEAD_DIM // 4
ROPE_THETA = 10000.0
EPS = 1e-6

LANES = 128
MXU_DIM = 256
VT_ROWS = HEAD_DIM + 16
VMEM_LIMIT = 56 * 1024 * 1024

LOG2_E = 1.4426950408889634
NEG_BIG = -1e30

TM_PROJ = 1024
TQ = 256
TK = 256
TILES_PER_ITER = 4
FINISH_AT_CHUNK = 1
TM_FFN = 512
FF_CHUNK = 256

F32 = jnp.float32
BF16 = jnp.bfloat16


def _rms(x):
    return lax.rsqrt(jnp.mean(x * x, axis=-1, keepdims=True) + EPS)


def _inproj_kernel(x_ref, w_ref, hg_ref, e_ref, cos_ref, sin_ref,
                   qt_ref, k_ref, vt_ref, u_ref):
    x = x_ref[...]
    h = x.astype(BF16)
    r = _rms(x)
    e = e_ref[...]
    cos = cos_ref[...]
    sin = sin_ref[...]
    lane = lax.broadcasted_iota(jnp.int32, cos.shape, 1)
    first_half = (lane & ROPE_AXIS_PAIRS) == 0
    tq = qt_ref.shape[-1]

    def project(c0, width):
        return jnp.dot(h, w_ref[:, c0:c0 + width], preferred_element_type=F32) * r

    def head_norm_rope(p, c0):
        width = p.shape[1]
        ss = jnp.dot((p * p).astype(BF16), e[:width, :width], preferred_element_type=F32)
        pn = p * lax.rsqrt(ss * (1.0 / HEAD_DIM) + EPS) * hg_ref[:, c0:c0 + width]
        out = []
        for c in range(width // LANES):
            xc = pn[:, c * LANES:(c + 1) * LANES]
            partner = jnp.where(first_half,
                                pltpu.roll(xc, LANES - ROPE_AXIS_PAIRS, 1),
                                pltpu.roll(xc, ROPE_AXIS_PAIRS, 1))
            out.append(xc * cos + partner * sin)
        return out

    def q_epilogue(p, blk):
        roped = head_norm_rope(p, blk * MXU_DIM)
        for c, rc in enumerate(roped):
            qt = rc.T.astype(BF16)
            for t in range(qt_ref.shape[2]):
                qt_ref[0, blk * (MXU_DIM // LANES) + c, t] = qt[:, t * tq:(t + 1) * tq]

    def kv_epilogue(kv):
        kc, = head_norm_rope(kv[:, :D_KV], D_ATTN)
        k_ref[:, 0:LANES] = kc.astype(BF16)
        k_ref[:, LANES:2 * LANES] = pltpu.roll(kc, HEAD_DIM, 1).astype(BF16)
        vt = kv[:, D_KV:].T
        ones = jnp.ones((VT_ROWS - HEAD_DIM, vt.shape[1]), BF16)
        for g in range(N_KV_HEADS):
            vt_ref[0, g, 0:HEAD_DIM, :] = vt[g * HEAD_DIM:(g + 1) * HEAD_DIM].astype(BF16)
            vt_ref[0, g, HEAD_DIM:VT_ROWS, :] = ones

    p_q0 = project(0, MXU_DIM)
    p_q1 = project(MXU_DIM, MXU_DIM)
    q_epilogue(p_q0, 0)
    p_kv = project(D_ATTN, 2 * D_KV)
    q_epilogue(p_q1, 1)
    u_ref[:, 0:MXU_DIM] = project(D_QK + D_KV, MXU_DIM)
    kv_epilogue(p_kv)
    u_ref[:, MXU_DIM:] = project(D_QK + D_KV + MXU_DIM, MXU_DIM)


def _inproj(x2d, w, hg, e, cos, sin, n_batch, tm, tq):
    n_rows = x2d.shape[0]
    tiles = n_rows // tm
    tpb = tiles // n_batch
    n_pos = cos.shape[0] // tm
    qpt = tm // tq
    n_pairs = N_HEADS // 2
    const = lambda i: (0, 0)
    return pl.pallas_call(
        _inproj_kernel,
        grid=(tiles,),
        in_specs=[
            pl.BlockSpec((tm, D_MODEL), lambda i: (i, 0)),
            pl.BlockSpec((D_MODEL, D_IN), const),
            pl.BlockSpec((1, D_QK), const),
            pl.BlockSpec((MXU_DIM, MXU_DIM), const),
            pl.BlockSpec((tm, LANES), lambda i: (i % n_pos, 0)),
            pl.BlockSpec((tm, LANES), lambda i: (i % n_pos, 0)),
        ],
        out_specs=[
            pl.BlockSpec((1, n_pairs, qpt, LANES, tq), lambda i: (i // tpb, 0, i % tpb, 0, 0)),
            pl.BlockSpec((tm, 2 * D_KV), lambda i: (i, 0)),
            pl.BlockSpec((1, N_KV_HEADS, VT_ROWS, tm), lambda i: (i // tpb, 0, 0, i % tpb)),
            pl.BlockSpec((tm, D_POOL), lambda i: (i, 0)),
        ],
        out_shape=[
            jax.ShapeDtypeStruct((n_batch, n_pairs, tpb * qpt, LANES, tq), BF16),
            jax.ShapeDtypeStruct((n_rows, 2 * D_KV), BF16),
            jax.ShapeDtypeStruct((n_batch, N_KV_HEADS, VT_ROWS, tpb * tm), BF16),
            jax.ShapeDtypeStruct((n_rows, D_POOL), F32),
        ],
        compiler_params=pltpu.CompilerParams(
            dimension_semantics=("parallel",), vmem_limit_bytes=VMEM_LIMIT),
        name="inproj",
    )(x2d, w, hg, e, cos, sin)


def _attn_kernel(q_ref, k_ref, vt_ref, kmeta_ref, vtmeta_ref,
                 wo_ref, gmix_ref, wi_ref, gffn_ref, wd_ref,
                 o_ref, wo_out, wi_out, wd_out, kx_ref, vx_ref, s_ref,
                 *, tq, tk, seq):
    wo_out[...] = (wo_ref[...] * gmix_ref[...]).astype(BF16)
    wi_out[...] = (wi_ref[...] * gffn_ref[...]).astype(BF16)
    wd_out[...] = wd_ref[...].astype(BF16)

    n_qt = seq // tq
    nq = 2 * tq
    n_chunks = seq // tk
    n_keys = seq + N_META
    bounds = [(c * tk, (c + 1) * tk) for c in range(n_chunks)]
    bounds[-1] = (bounds[-1][0], n_keys)

    kx_ref[0:seq, :] = k_ref[...]
    kx_ref[seq:n_keys, :] = kmeta_ref[...]
    vx_ref[:, 0:seq] = vt_ref[0, 0]
    vx_ref[:, seq:seq + LANES] = vtmeta_ref[0]

    def weights(j, pair):
        qt = q_ref[0, pair, j]
        z = jnp.zeros((HEAD_DIM, tq), BF16)
        return jnp.concatenate([jnp.concatenate([qt[0:HEAD_DIM], z], axis=0),
                                jnp.concatenate([qt[HEAD_DIM:], z], axis=0)], axis=1)

    def score_chunk(w, slot, c, mx):
        lo, hi = bounds[c]
        s = jnp.dot(kx_ref[lo:hi, :], w, preferred_element_type=F32)
        for head in range(2):
            s_ref[head, lo:hi, slot * tq:(slot + 1) * tq] = s[:, head * tq:(head + 1) * tq]
        return jnp.maximum(mx, jnp.max(s, axis=0, keepdims=True))

    def pv_chunk(slot, c, m, acc):
        lo, hi = bounds[c]
        s = jnp.concatenate([s_ref[head, lo:hi, slot * tq:(slot + 1) * tq] for head in range(2)],
                            axis=1)
        p = jnp.exp2(s - m).astype(BF16)
        if hi == n_keys:
            hi = seq + LANES
            p = jnp.concatenate([p, jnp.zeros((hi - n_keys, nq), BF16)], axis=0)
        return acc + jnp.dot(vx_ref[:, lo:hi], p, preferred_element_type=F32)

    def stage(w_next, slot_next, slot_cur, m_cur, finish_prev):
        mx = jnp.full((1, nq), NEG_BIG, F32)
        acc = jnp.zeros((VT_ROWS, nq), F32)
        for c in range(n_chunks):
            mx = score_chunk(w_next, slot_next, c, mx)
            acc = pv_chunk(slot_cur, c, m_cur, acc)
            if c == FINISH_AT_CHUNK and finish_prev is not None:
                finish_prev()
        return mx, acc

    def finish(acc, j, pair):
        o = acc[0:HEAD_DIM] * (1.0 / acc[HEAD_DIM:HEAD_DIM + 1])
        both = jnp.concatenate([o[:, :tq], o[:, tq:]], axis=0)
        o_ref[pl.ds(pl.multiple_of(j * tq, tq), tq), pair * LANES:(pair + 1) * LANES] = both.T

    w0 = weights(0, 0)
    m0 = jnp.full((1, nq), NEG_BIG, F32)
    for c in range(n_chunks):
        m0 = score_chunk(w0, 0, c, m0)

    def body(jj, m_cur):
        finish_prev = None
        for t in range(TILES_PER_ITER):
            j = jj * TILES_PER_ITER + t
            for pair in range(2):
                nxt = (j, 1) if pair == 0 else (jnp.minimum(j + 1, n_qt - 1), 0)
                m_next, acc = stage(weights(*nxt), 1 - pair, pair, m_cur, finish_prev)
                finish_prev = functools.partial(finish, acc, j, pair)
                m_cur = m_next
        finish_prev()
        return m_cur

    lax.fori_loop(0, n_qt // TILES_PER_ITER, body, m0)


def _attention(qt, k2, vt, kmeta2, vtmeta, w_out, g_mix, w_ffn_in, g_ffn, w_ffn_down,
               n_batch, tq, tk):
    n_rows = k2.shape[0]
    seq = n_rows // n_batch
    n_qt = seq // tq
    gw = KV_GROUP * HEAD_DIM
    n_steps = n_batch * N_KV_HEADS
    kernel = functools.partial(_attn_kernel, tq=tq, tk=tk, seq=seq)

    def slab(w):
        rows = w.shape[0] // n_steps
        assert rows * n_steps == w.shape[0] and rows % 16 == 0, w.shape
        return pl.BlockSpec((rows, w.shape[1]), lambda b, g: (b * N_KV_HEADS + g, 0))

    weights_in = (w_out, g_mix, w_ffn_in, g_ffn, w_ffn_down)
    weights_out = (w_out, w_ffn_in, w_ffn_down)
    return pl.pallas_call(
        kernel,
        grid=(n_batch, N_KV_HEADS),
        in_specs=[
            pl.BlockSpec((1, KV_GROUP // 2, n_qt, LANES, tq), lambda b, g: (b, g, 0, 0, 0)),
            pl.BlockSpec((seq, LANES), lambda b, g: (b, g)),
            pl.BlockSpec((1, 1, VT_ROWS, seq), lambda b, g: (b, g, 0, 0)),
            pl.BlockSpec((N_META, LANES), lambda b, g: (0, g)),
            pl.BlockSpec((1, VT_ROWS, LANES), lambda b, g: (g, 0, 0)),
        ] + [slab(w) for w in weights_in],
        out_specs=[pl.BlockSpec((seq, gw), lambda b, g: (b, g))] + [slab(w) for w in weights_out],
        out_shape=[jax.ShapeDtypeStruct((n_rows, D_ATTN), F32)]
        + [jax.ShapeDtypeStruct(w.shape, BF16) for w in weights_out],
        scratch_shapes=[pltpu.VMEM((seq + LANES, LANES), BF16),
                        pltpu.VMEM((VT_ROWS, seq + LANES), BF16),
                        pltpu.VMEM((2, seq + LANES, 2 * tq), F32)],
        compiler_params=pltpu.CompilerParams(
            dimension_semantics=("parallel", "parallel"), vmem_limit_bytes=VMEM_LIMIT),
        name="attention",
    )(qt, k2, vt, kmeta2, vtmeta, *weights_in)


def _mix_ffn_kernel(x_ref, a_ref, u_ref, uprev_ref, unext_ref, umeta_ref,
                    wp_ref, wo_ref, wi_ref, wd_ref, gl_ref,
                    o_ref, ext_ref, act_ref, *, tm, tpb, seq_total):
    j = pl.program_id(0) % tpb

    a = a_ref[...]
    proj_a = jnp.dot(a.astype(BF16), wo_ref[0:D_ATTN, :], preferred_element_type=F32) * _rms(a)

    ext_ref[POOL_HALO:POOL_HALO + tm, :] = u_ref[...]
    ext_ref[0:POOL_HALO, :] = jnp.where(j == 0, umeta_ref[N_META - POOL_HALO:N_META, :],
                                        uprev_ref[...])
    ext_ref[POOL_HALO + tm:2 * POOL_HALO + tm, :] = jnp.where(j == tpb - 1, 0.0, unext_ref[...])
    ext_ref[2 * POOL_HALO + tm:, :] = jnp.zeros((EXT_SLACK, D_POOL), F32)

    t8 = j * tm + (tm - POOL_HALO + N_META) + lax.broadcasted_iota(
        jnp.int32, (POOL_HALO, POOL_GROUP), 0)
    diffs = []
    for gi, w in enumerate(POOL_WINDOWS):
        half = w // 2
        xg = ext_ref[:, gi * POOL_GROUP:(gi + 1) * POOL_GROUP]
        s, k, n = xg, 1, tm + 2 * POOL_HALO + EXT_SLACK
        while k < half:
            n -= POOL_HALO
            s = s[0:n] + s[k:k + n]
            k *= 2
        win = s[POOL_HALO - half:POOL_HALO - half + tm] + s[POOL_HALO:POOL_HALO + tm]
        u_self = xg[POOL_HALO:POOL_HALO + tm]
        cnt8 = jnp.minimum(t8 + half, seq_total) - (t8 - half)
        body = win[:tm - POOL_HALO] * (1.0 / w) - u_self[:tm - POOL_HALO]
        tail = win[tm - POOL_HALO:] * (1.0 / cnt8.astype(F32)) - u_self[tm - POOL_HALO:]
        diffs.append(jnp.concatenate([body, tail], axis=0))
    pooled = []
    for pr in range(2):
        d = jnp.concatenate(diffs[2 * pr:2 * pr + 2], axis=1).astype(BF16)
        pooled.append(jnp.dot(d, wp_ref[pr], preferred_element_type=F32))
    y = jnp.concatenate(pooled, axis=1)
    proj_p = jnp.dot(y.astype(BF16), wo_ref[D_ATTN:, :], preferred_element_type=F32) * _rms(y)
    h1 = x_ref[...] + proj_a + proj_p

    h1b = h1.astype(BF16)
    r1 = _rms(h1)
    for c in range(D_FF // FF_CHUNK):
        cols = slice(c * FF_CHUNK, (c + 1) * FF_CHUNK)
        gate = jnp.dot(h1b, wi_ref[:, cols], preferred_element_type=F32) * r1
        up = jnp.dot(h1b, wi_ref[:, D_FF + c * FF_CHUNK:D_FF + (c + 1) * FF_CHUNK],
                     preferred_element_type=F32) * r1
        act_ref[:, cols] = (gate * jax.nn.sigmoid(gate) * up).astype(BF16)
    h3 = h1 + jnp.dot(act_ref[...], wd_ref[...], preferred_element_type=F32)
    o_ref[...] = h3 * _rms(h3) * gl_ref[...]


def _mix_ffn(x2d, attn, u, umeta, wp, wo, wi, wd, gl, n_batch, tm):
    n_rows = x2d.shape[0]
    tiles = n_rows // tm
    tpb = tiles // n_batch
    hb = tm // POOL_HALO
    n_hblocks = n_rows // POOL_HALO
    kernel = functools.partial(_mix_ffn_kernel, tm=tm, tpb=tpb,
                               seq_total=N_META + n_rows // n_batch)

    def const(shape):
        zeros = (0,) * len(shape)
        return pl.BlockSpec(shape, lambda i: zeros, pipeline_mode=pl.Buffered(1))

    return pl.pallas_call(
        kernel,
        grid=(tiles,),
        in_specs=[
            pl.BlockSpec((tm, D_MODEL), lambda i: (i, 0)),
            pl.BlockSpec((tm, D_ATTN), lambda i: (i, 0)),
            pl.BlockSpec((tm, D_POOL), lambda i: (i, 0)),
            pl.BlockSpec((POOL_HALO, D_POOL), lambda i: (jnp.maximum(i * hb - 1, 0), 0)),
            pl.BlockSpec((POOL_HALO, D_POOL),
                         lambda i: (jnp.minimum((i + 1) * hb, n_hblocks - 1), 0)),
            const((N_META, D_POOL)),
            const((2, MXU_DIM, MXU_DIM)),
            const((D_MODEL, D_MODEL)),
            const((D_MODEL, 2 * D_FF)),
            const((D_FF, D_MODEL)),
            const((1, D_MODEL)),
        ],
        out_specs=pl.BlockSpec((tm, D_MODEL), lambda i: (i, 0)),
        out_shape=jax.ShapeDtypeStruct((n_rows, D_MODEL), F32),
        scratch_shapes=[pltpu.VMEM((tm + 2 * POOL_HALO + EXT_SLACK, D_POOL), F32),
                        pltpu.VMEM((tm, D_FF), BF16)],
        compiler_params=pltpu.CompilerParams(
            dimension_semantics=("parallel",), vmem_limit_bytes=VMEM_LIMIT),
        name="mix_ffn",
    )(x2d, attn, u, u, u, umeta, wp, wo, wi, wd, gl)


def _rope_tables(n_real):
    rows = n_real // GRID_W
    r = np.arange(n_real)
    row = (r // GRID_W - rows // 2).astype(np.float64)
    col = (r % GRID_W - GRID_W // 2).astype(np.float64)
    freqs = ROPE_THETA ** (-np.arange(ROPE_AXIS_PAIRS, dtype=np.float64) / ROPE_AXIS_PAIRS)
    ar = row[:, None] * freqs
    ac = col[:, None] * freqs
    cos = np.concatenate([np.cos(ar), np.cos(ar), np.cos(ac), np.cos(ac)], axis=1)
    sin = np.concatenate([-np.sin(ar), np.sin(ar), -np.sin(ac), np.sin(ac)], axis=1)
    return (jnp.asarray(np.tile(cos, (1, 2)), dtype=F32),
            jnp.asarray(np.tile(sin, (1, 2)), dtype=F32))


def kernel(x, meta_tokens, norm_mix, w_in, q_norm, k_norm, attn_out_norm, w_pool,
           pool_scale, w_out, norm_ffn, w_ffn_in, w_ffn_down, norm_final):
    n_batch, n_real, _ = x.shape
    assert n_real % TM_PROJ == 0 and n_real % TQ == 0 and n_real % TM_FFN == 0
    assert w_in.shape[0] == 1, "single layer"
    x2d = x.reshape(n_batch * n_real, D_MODEL)

    w_in_b = (norm_mix[0][:, None] * w_in[0]).astype(BF16)
    head_gain = jnp.concatenate([jnp.tile(q_norm[0], N_HEADS) * (HEAD_DIM ** -0.5 * LOG2_E),
                                 jnp.tile(k_norm[0], N_KV_HEADS)]).reshape(1, D_QK)
    blk = np.arange(MXU_DIM) // HEAD_DIM
    e_heads = jnp.asarray(blk[:, None] == blk[None, :], dtype=BF16)
    cos, sin = _rope_tables(n_real)

    wp = jnp.zeros((2, MXU_DIM, MXU_DIM), F32)
    for gi in range(len(POOL_WINDOWS)):
        r0 = (gi % 2) * POOL_GROUP
        wp = wp.at[gi // 2, r0:r0 + POOL_GROUP, r0:r0 + POOL_GROUP].set(w_pool[0, gi])
    wp = wp.astype(BF16)
    g_mix = jnp.concatenate([attn_out_norm[0], pool_scale[0]]).reshape(D_MODEL, 1)
    g_ffn = norm_ffn[0].reshape(D_MODEL, 1)

    qt, k2, vt, u = _inproj(x2d, w_in_b, head_gain, e_heads, cos, sin, n_batch, TM_PROJ, TQ)
    meta_pad = jnp.zeros((TQ, D_MODEL), F32).at[:N_META].set(meta_tokens)
    ident_cos = jnp.asarray(np.ones((TQ, LANES), np.float32))
    ident_sin = jnp.asarray(np.zeros((TQ, LANES), np.float32))
    _, kmeta2, vtmeta, umeta = _inproj(meta_pad, w_in_b, head_gain, e_heads,
                                       ident_cos, ident_sin, 1, TQ, TQ)
    kmeta2 = kmeta2[:N_META]
    lane = jnp.arange(LANES)
    vtmeta = jnp.where(lane[None, None, :] < N_META, vtmeta[0, :, :, :LANES], 0).astype(BF16)
    umeta = umeta[:N_META]

    attn, w_out_b, wi, wd = _attention(qt, k2, vt, kmeta2, vtmeta,
                                       w_out[0], g_mix, w_ffn_in[0], g_ffn, w_ffn_down[0],
                                       n_batch, TQ, TK)

    out = _mix_ffn(x2d, attn, u, umeta, wp, w_out_b, wi, wd,
                   norm_final.reshape(1, D_MODEL), n_batch, TM_FFN)
    return out.reshape(n_batch, n_real, D_MODEL)
```

```python
import functools

import jax
import jax.numpy as jnp
import numpy as np
from jax import lax
from jax.experimental import pallas as pl
from jax.experimental.pallas import tpu as pltpu

D_MODEL = 1024
N_META = 16
GRID_W = 64
N_HEADS = 8
N_KV_HEADS = 2
HEAD_DIM = 64
KV_GROUP = N_HEADS // N_KV_HEADS
D_ATTN = N_HEADS * HEAD_DIM
D_KV = N_KV_HEADS * HEAD_DIM
D_QK = D_ATTN + D_KV
D_POOL = 512
POOL_WINDOWS = (2, 4, 8, 16)
POOL_GROUP = 128
POOL_HALO = 8
EXT_SLACK = 16
D_IN = D_ATTN + 2 * D_KV + D_POOL
D_FF = 2816
ROPE_AXIS_PAIRS = HEAD_DIM // 4
ROPE_THETA = 10000.0
EPS = 1e-6

LANES = 128
MXU_DIM = 256
VT_ROWS = HEAD_DIM + 16
VMEM_LIMIT = 56 * 1024 * 1024

LOG2_E = 1.4426950408889634

TM_PROJ = 1024
TQ = 256
TK = 256
TILES_PER_ITER = 4
TM_FFN = 512
FF_CHUNK = 256

F32 = jnp.float32
BF16 = jnp.bfloat16


def _rms(x):
    return lax.rsqrt(jnp.mean(x * x, axis=-1, keepdims=True) + EPS)


def _inproj_kernel(x_ref, w_ref, hg_ref, e_ref, cos_ref, sin_ref,
                   qt_ref, k_ref, vt_ref, u_ref):
    x = x_ref[...]
    h = x.astype(BF16)
    r = _rms(x)
    e = e_ref[...]
    cos = cos_ref[...]
    sin = sin_ref[...]
    lane = lax.broadcasted_iota(jnp.int32, cos.shape, 1)
    first_half = (lane & ROPE_AXIS_PAIRS) == 0
    tq = qt_ref.shape[-1]

    def project(c0, width):
        return jnp.dot(h, w_ref[:, c0:c0 + width], preferred_element_type=F32) * r

    def head_norm_rope(p, c0):
        width = p.shape[1]
        ss = jnp.dot((p * p).astype(BF16), e[:width, :width], preferred_element_type=F32)
        pn = p * lax.rsqrt(ss * (1.0 / HEAD_DIM) + EPS) * hg_ref[:, c0:c0 + width]
        out = []
        for c in range(width // LANES):
            xc = pn[:, c * LANES:(c + 1) * LANES]
            partner = jnp.where(first_half,
                                pltpu.roll(xc, LANES - ROPE_AXIS_PAIRS, 1),
                                pltpu.roll(xc, ROPE_AXIS_PAIRS, 1))
            out.append(xc * cos + partner * sin)
        return out

    def q_epilogue(p, blk):
        roped = head_norm_rope(p, blk * MXU_DIM)
        for c, rc in enumerate(roped):
            qt = rc.T.astype(BF16)
            for t in range(qt_ref.shape[2]):
                qt_ref[0, blk * (MXU_DIM // LANES) + c, t] = qt[:, t * tq:(t + 1) * tq]

    def kv_epilogue(kv):
        kc, = head_norm_rope(kv[:, :D_KV], D_ATTN)
        k_ref[:, 0:LANES] = kc.astype(BF16)
        k_ref[:, LANES:2 * LANES] = pltpu.roll(kc, HEAD_DIM, 1).astype(BF16)
        vt = kv[:, D_KV:].T
        ones = jnp.ones((VT_ROWS - HEAD_DIM, vt.shape[1]), BF16)
        for g in range(N_KV_HEADS):
            vt_ref[0, g, 0:HEAD_DIM, :] = vt[g * HEAD_DIM:(g + 1) * HEAD_DIM].astype(BF16)
            vt_ref[0, g, HEAD_DIM:VT_ROWS, :] = ones

    p_q0 = project(0, MXU_DIM)
    p_q1 = project(MXU_DIM, MXU_DIM)
    q_epilogue(p_q0, 0)
    p_kv = project(D_ATTN, 2 * D_KV)
    q_epilogue(p_q1, 1)
    u_ref[:, 0:MXU_DIM] = project(D_QK + D_KV, MXU_DIM)
    kv_epilogue(p_kv)
    u_ref[:, MXU_DIM:] = project(D_QK + D_KV + MXU_DIM, MXU_DIM)


def _inproj(x2d, w, hg, e, cos, sin, n_batch, tm, tq):
    n_rows = x2d.shape[0]
    tiles = n_rows // tm
    tpb = tiles // n_batch
    n_pos = cos.shape[0] // tm
    qpt = tm // tq
    n_pairs = N_HEADS // 2
    const = lambda i: (0, 0)
    return pl.pallas_call(
        _inproj_kernel,
        grid=(tiles,),
        in_specs=[
            pl.BlockSpec((tm, D_MODEL), lambda i: (i, 0)),
            pl.BlockSpec((D_MODEL, D_IN), const),
            pl.BlockSpec((1, D_QK), const),
            pl.BlockSpec((MXU_DIM, MXU_DIM), const),
            pl.BlockSpec((tm, LANES), lambda i: (i % n_pos, 0)),
            pl.BlockSpec((tm, LANES), lambda i: (i % n_pos, 0)),
        ],
        out_specs=[
            pl.BlockSpec((1, n_pairs, qpt, LANES, tq), lambda i: (i // tpb, 0, i % tpb, 0, 0)),
            pl.BlockSpec((tm, 2 * D_KV), lambda i: (i, 0)),
            pl.BlockSpec((1, N_KV_HEADS, VT_ROWS, tm), lambda i: (i // tpb, 0, 0, i % tpb)),
            pl.BlockSpec((tm, D_POOL), lambda i: (i, 0)),
        ],
        out_shape=[
            jax.ShapeDtypeStruct((n_batch, n_pairs, tpb * qpt, LANES, tq), BF16),
            jax.ShapeDtypeStruct((n_rows, 2 * D_KV), BF16),
            jax.ShapeDtypeStruct((n_batch, N_KV_HEADS, VT_ROWS, tpb * tm), BF16),
            jax.ShapeDtypeStruct((n_rows, D_POOL), F32),
        ],
        compiler_params=pltpu.CompilerParams(
            dimension_semantics=("parallel",), vmem_limit_bytes=VMEM_LIMIT),
        name="inproj",
    )(x2d, w, hg, e, cos, sin)


def _attn_kernel(q_ref, k_ref, vt_ref, kmeta_ref, vtmeta_ref,
                 wo_ref, gmix_ref, wi_ref, gffn_ref, wd_ref,
                 o_ref, wo_out, wi_out, wd_out, kx_ref, vx_ref, s_ref,
                 *, tq, tk, seq):
    wo_out[...] = (wo_ref[...] * gmix_ref[...]).astype(BF16)
    wi_out[...] = (wi_ref[...] * gffn_ref[...]).astype(BF16)
    wd_out[...] = wd_ref[...].astype(BF16)

    n_qt = seq // tq
    nq = 2 * tq
    n_chunks = seq // tk
    n_keys = seq + N_META
    bounds = [(c * tk, (c + 1) * tk) for c in range(n_chunks)]
    bounds[-1] = (bounds[-1][0], n_keys)

    last_lo = bounds[-1][0]
    kx_ref[0:seq - last_lo, :] = k_ref[last_lo:seq, :]
    kx_ref[seq - last_lo:n_keys - last_lo, :] = kmeta_ref[...]
    vx_ref[:, 0:seq - last_lo] = vt_ref[0, 0, :, last_lo:seq]
    vx_ref[:, seq - last_lo:seq - last_lo + LANES] = vtmeta_ref[0]

    def keys(c):
        lo, hi = bounds[c]
        return kx_ref[0:hi - lo, :] if c == n_chunks - 1 else k_ref[lo:hi, :]

    def values_t(c):
        lo, hi = bounds[c]
        return vx_ref[...] if c == n_chunks - 1 else vt_ref[0, 0, :, lo:hi]

    def weights(j, pair):
        qt = q_ref[0, pair, j]
        z = jnp.zeros((HEAD_DIM, tq), BF16)
        return jnp.concatenate([jnp.concatenate([qt[0:HEAD_DIM], z], axis=0),
                                jnp.concatenate([qt[HEAD_DIM:], z], axis=0)], axis=1)

    def score_chunk(w, slot, c, mx):
        lo, hi = bounds[c]
        s = jnp.dot(keys(c), w, preferred_element_type=F32)
        s_ref[slot, lo:hi, :] = s
        return jnp.maximum(mx, jnp.max(s, axis=0, keepdims=True))

    def pv_chunk(slot, c, m, acc):
        lo, hi = bounds[c]
        p = jnp.exp2(s_ref[slot, lo:hi, :] - m).astype(BF16)
        vt_c = values_t(c)
        if p.shape[0] < vt_c.shape[1]:
            p = jnp.concatenate([p, jnp.zeros((vt_c.shape[1] - p.shape[0], nq), BF16)], axis=0)
        return acc + jnp.dot(vt_c, p, preferred_element_type=F32)

    def stage(w_next, slot_next, slot_cur, m_cur):
        mx = jnp.full((1, nq), -jnp.inf, F32)
        acc = jnp.zeros((VT_ROWS, nq), F32)
        for c in range(n_chunks):
            mx = score_chunk(w_next, slot_next, c, mx)
            acc = pv_chunk(slot_cur, c, m_cur, acc)
        return mx, acc

    def finish(acc, j, pair):
        o = acc[0:HEAD_DIM] * (1.0 / acc[HEAD_DIM:HEAD_DIM + 1])
        both = jnp.concatenate([o[:, :tq], o[:, tq:]], axis=0)
        o_ref[pl.ds(pl.multiple_of(j * tq, tq), tq), pair * LANES:(pair + 1) * LANES] = both.T

    w0 = weights(0, 0)
    m0 = jnp.full((1, nq), -jnp.inf, F32)
    for c in range(n_chunks):
        m0 = score_chunk(w0, 0, c, m0)

    def body(jj, m_cur):
        for t in range(TILES_PER_ITER):
            j = jj * TILES_PER_ITER + t
            for pair in range(2):
                nxt = (j, 1) if pair == 0 else (jnp.minimum(j + 1, n_qt - 1), 0)
                m_cur, acc = stage(weights(*nxt), 1 - pair, pair, m_cur)
                finish(acc, j, pair)
        return m_cur

    lax.fori_loop(0, n_qt // TILES_PER_ITER, body, m0)


def _attention(qt, k2, vt, kmeta2, vtmeta, w_out, g_mix, w_ffn_in, g_ffn, w_ffn_down,
               n_batch, tq, tk):
    n_rows = k2.shape[0]
    seq = n_rows // n_batch
    n_qt = seq // tq
    gw = KV_GROUP * HEAD_DIM
    n_steps = n_batch * N_KV_HEADS
    kernel = functools.partial(_attn_kernel, tq=tq, tk=tk, seq=seq)

    def slab(w):
        rows = w.shape[0] // n_steps
        assert rows * n_steps == w.shape[0] and rows % 16 == 0, w.shape
        return pl.BlockSpec((rows, w.shape[1]), lambda b, g: (b * N_KV_HEADS + g, 0))

    weights_in = (w_out, g_mix, w_ffn_in, g_ffn, w_ffn_down)
    weights_out = (w_out, w_ffn_in, w_ffn_down)
    return pl.pallas_call(
        kernel,
        grid=(n_batch, N_KV_HEADS),
        in_specs=[
            pl.BlockSpec((1, KV_GROUP // 2, n_qt, LANES, tq), lambda b, g: (b, g, 0, 0, 0)),
            pl.BlockSpec((seq, LANES), lambda b, g: (b, g)),
            pl.BlockSpec((1, 1, VT_ROWS, seq), lambda b, g: (b, g, 0, 0)),
            pl.BlockSpec((N_META, LANES), lambda b, g: (0, g)),
            pl.BlockSpec((1, VT_ROWS, LANES), lambda b, g: (g, 0, 0)),
        ] + [slab(w) for w in weights_in],
        out_specs=[pl.BlockSpec((seq, gw), lambda b, g: (b, g))] + [slab(w) for w in weights_out],
        out_shape=[jax.ShapeDtypeStruct((n_rows, D_ATTN), F32)]
        + [jax.ShapeDtypeStruct(w.shape, BF16) for w in weights_out],
        scratch_shapes=[pltpu.VMEM((tk + N_META, LANES), BF16),
                        pltpu.VMEM((VT_ROWS, tk + LANES), BF16),
                        pltpu.VMEM((2, seq + N_META, 2 * tq), F32)],
        compiler_params=pltpu.CompilerParams(
            dimension_semantics=("parallel", "parallel"), vmem_limit_bytes=VMEM_LIMIT),
        name="attention",
    )(qt, k2, vt, kmeta2, vtmeta, *weights_in)


def _mix_ffn_kernel(x_ref, a_ref, u_ref, uprev_ref, unext_ref, umeta_ref,
                    wp_ref, wo_ref, wi_ref, wd_ref, gl_ref,
                    o_ref, ext_ref, act_ref, *, tm, tpb, seq_total):
    j = pl.program_id(0) % tpb

    a = a_ref[...]
    proj_a = jnp.dot(a.astype(BF16), wo_ref[0:D_ATTN, :], preferred_element_type=F32) * _rms(a)

    ext_ref[POOL_HALO:POOL_HALO + tm, :] = u_ref[...]
    ext_ref[0:POOL_HALO, :] = jnp.where(j == 0, umeta_ref[N_META - POOL_HALO:N_META, :],
                                        uprev_ref[...])
    ext_ref[POOL_HALO + tm:2 * POOL_HALO + tm, :] = jnp.where(j == tpb - 1, 0.0, unext_ref[...])
    ext_ref[2 * POOL_HALO + tm:, :] = jnp.zeros((EXT_SLACK, D_POOL), F32)

    t8 = j * tm + (tm - POOL_HALO + N_META) + lax.broadcasted_iota(
        jnp.int32, (POOL_HALO, POOL_GROUP), 0)
    diffs = []
    for gi, w in enumerate(POOL_WINDOWS):
        half = w // 2
        xg = ext_ref[:, gi * POOL_GROUP:(gi + 1) * POOL_GROUP]
        s, k, n = xg, 1, tm + 2 * POOL_HALO + EXT_SLACK
        while k < half:
            n -= POOL_HALO
            s = s[0:n] + s[k:k + n]
            k *= 2
        win = s[POOL_HALO - half:POOL_HALO - half + tm] + s[POOL_HALO:POOL_HALO + tm]
        u_self = xg[POOL_HALO:POOL_HALO + tm]
        cnt8 = jnp.minimum(t8 + half, seq_total) - (t8 - half)
        body = win[:tm - POOL_HALO] * (1.0 / w) - u_self[:tm - POOL_HALO]
        tail = win[tm - POOL_HALO:] * (1.0 / cnt8.astype(F32)) - u_self[tm - POOL_HALO:]
        diffs.append(jnp.concatenate([body, tail], axis=0))
    pooled = []
    for pr in range(2):
        d = jnp.concatenate(diffs[2 * pr:2 * pr + 2], axis=1).astype(BF16)
        pooled.append(jnp.dot(d, wp_ref[pr], preferred_element_type=F32))
    y = jnp.concatenate(pooled, axis=1)
    proj_p = jnp.dot(y.astype(BF16), wo_ref[D_ATTN:, :], preferred_element_type=F32) * _rms(y)
    h1 = x_ref[...] + proj_a + proj_p

    h1b = h1.astype(BF16)
    r1 = _rms(h1)
    for c in range(D_FF // FF_CHUNK):
        cols = slice(c * FF_CHUNK, (c + 1) * FF_CHUNK)
        gate = jnp.dot(h1b, wi_ref[:, cols], preferred_element_type=F32) * r1
        up = jnp.dot(h1b, wi_ref[:, D_FF + c * FF_CHUNK:D_FF + (c + 1) * FF_CHUNK],
                     preferred_element_type=F32) * r1
        act_ref[:, cols] = (gate * jax.nn.sigmoid(gate) * up).astype(BF16)
    h3 = h1 + jnp.dot(act_ref[...], wd_ref[...], preferred_element_type=F32)
    o_ref[...] = h3 * _rms(h3) * gl_ref[...]


def _mix_ffn(x2d, attn, u, umeta, wp, wo, wi, wd, gl, n_batch, tm):
    n_rows = x2d.shape[0]
    tiles = n_rows // tm
    tpb = tiles // n_batch
    hb = tm // POOL_HALO
    n_hblocks = n_rows // POOL_HALO
    kernel = functools.partial(_mix_ffn_kernel, tm=tm, tpb=tpb,
                               seq_total=N_META + n_rows // n_batch)

    def const(shape):
        zeros = (0,) * len(shape)
        return pl.BlockSpec(shape, lambda i: zeros, pipeline_mode=pl.Buffered(1))

    return pl.pallas_call(
        kernel,
        grid=(tiles,),
        in_specs=[
            pl.BlockSpec((tm, D_MODEL), lambda i: (i, 0)),
            pl.BlockSpec((tm, D_ATTN), lambda i: (i, 0)),
            pl.BlockSpec((tm, D_POOL), lambda i: (i, 0)),
            pl.BlockSpec((POOL_HALO, D_POOL), lambda i: (jnp.maximum(i * hb - 1, 0), 0)),
            pl.BlockSpec((POOL_HALO, D_POOL),
                         lambda i: (jnp.minimum((i + 1) * hb, n_hblocks - 1), 0)),
            const((N_META, D_POOL)),
            const((2, MXU_DIM, MXU_DIM)),
            const((D_MODEL, D_MODEL)),
            const((D_MODEL, 2 * D_FF)),
            const((D_FF, D_MODEL)),
            const((1, D_MODEL)),
        ],
        out_specs=pl.BlockSpec((tm, D_MODEL), lambda i: (i, 0)),
        out_shape=jax.ShapeDtypeStruct((n_rows, D_MODEL), F32),
        scratch_shapes=[pltpu.VMEM((tm + 2 * POOL_HALO + EXT_SLACK, D_POOL), F32),
                        pltpu.VMEM((tm, D_FF), BF16)],
        compiler_params=pltpu.CompilerParams(
            dimension_semantics=("parallel",), vmem_limit_bytes=VMEM_LIMIT),
        name="mix_ffn",
    )(x2d, attn, u, u, u, umeta, wp, wo, wi, wd, gl)


def _rope_tables(n_real):
    rows = n_real // GRID_W
    r = np.arange(n_real)
    row = (r // GRID_W - rows // 2).astype(np.float64)
    col = (r % GRID_W - GRID_W // 2).astype(np.float64)
    freqs = ROPE_THETA ** (-np.arange(ROPE_AXIS_PAIRS, dtype=np.float64) / ROPE_AXIS_PAIRS)
    ar = row[:, None] * freqs
    ac = col[:, None] * freqs
    cos = np.concatenate([np.cos(ar), np.cos(ar), np.cos(ac), np.cos(ac)], axis=1)
    sin = np.concatenate([-np.sin(ar), np.sin(ar), -np.sin(ac), np.sin(ac)], axis=1)
    return (jnp.asarray(np.tile(cos, (1, 2)), dtype=F32),
            jnp.asarray(np.tile(sin, (1, 2)), dtype=F32))


def kernel(x, meta_tokens, norm_mix, w_in, q_norm, k_norm, attn_out_norm, w_pool,
           pool_scale, w_out, norm_ffn, w_ffn_in, w_ffn_down, norm_final):
    n_batch, n_real, _ = x.shape
    assert n_real % TM_PROJ == 0 and n_real % TQ == 0 and n_real % TM_FFN == 0
    assert w_in.shape[0] == 1, "single layer"
    x2d = x.reshape(n_batch * n_real, D_MODEL)

    w_in_b = (norm_mix[0][:, None] * w_in[0]).astype(BF16)
    head_gain = jnp.concatenate([jnp.tile(q_norm[0], N_HEADS) * (HEAD_DIM ** -0.5 * LOG2_E),
                                 jnp.tile(k_norm[0], N_KV_HEADS)]).reshape(1, D_QK)
    blk = np.arange(MXU_DIM) // HEAD_DIM
    e_heads = jnp.asarray(blk[:, None] == blk[None, :], dtype=BF16)
    cos, sin = _rope_tables(n_real)

    wp = jnp.zeros((2, MXU_DIM, MXU_DIM), F32)
    for gi in range(len(POOL_WINDOWS)):
        r0 = (gi % 2) * POOL_GROUP
        wp = wp.at[gi // 2, r0:r0 + POOL_GROUP, r0:r0 + POOL_GROUP].set(w_pool[0, gi])
    wp = wp.astype(BF16)
    g_mix = jnp.concatenate([attn_out_norm[0], pool_scale[0]]).reshape(D_MODEL, 1)
    g_ffn = norm_ffn[0].reshape(D_MODEL, 1)

    qt, k2, vt, u = _inproj(x2d, w_in_b, head_gain, e_heads, cos, sin, n_batch, TM_PROJ, TQ)
    meta_pad = jnp.zeros((TQ, D_MODEL), F32).at[:N_META].set(meta_tokens)
    ident_cos = jnp.asarray(np.ones((TQ, LANES), np.float32))
    ident_sin = jnp.asarray(np.zeros((TQ, LANES), np.float32))
    _, kmeta2, vtmeta, umeta = _inproj(meta_pad, w_in_b, head_gain, e_heads,
                                       ident_cos, ident_sin, 1, TQ, TQ)
    kmeta2 = kmeta2[:N_META]
    lane = jnp.arange(LANES)
    vtmeta = jnp.where(lane[None, None, :] < N_META, vtmeta[0, :, :, :LANES], 0).astype(BF16)
    umeta = umeta[:N_META]

    attn, w_out_b, wi, wd = _attention(qt, k2, vt, kmeta2, vtmeta,
                                       w_out[0], g_mix, w_ffn_in[0], g_ffn, w_ffn_down[0],
                                       n_batch, TQ, TK)

    out = _mix_ffn(x2d, attn, u, umeta, wp, w_out_b, wi, wd,
                   norm_final.reshape(1, D_MODEL), n_batch, TM_FFN)
    return out.reshape(n_batch, n_real, D_MODEL)
```

```python
import functools

import jax
import jax.numpy as jnp
import numpy as np
from jax import lax
from jax.experimental import pallas as pl
from jax.experimental.pallas import tpu as pltpu

D_MODEL = 1024
N_META = 16
GRID_W = 64
N_HEADS = 8
N_KV_HEADS = 2
HEAD_DIM = 64
KV_GROUP = N_HEADS // N_KV_HEADS
D_ATTN = N_HEADS * HEAD_DIM
D_KV = N_KV_HEADS * HEAD_DIM
D_QK = D_ATTN + D_KV
D_POOL = 512
POOL_WINDOWS = (2, 4, 8, 16)
POOL_GROUP = 128
POOL_HALO = 8
EXT_SLACK = 16
D_IN = D_ATTN + 2 * D_KV + D_POOL
D_FF = 2816
ROPE_AXIS_PAIRS = HEAD_DIM // 4
ROPE_THETA = 10000.0
EPS = 1e-6

LANES = 128
MXU_DIM = 256
VT_ROWS = HEAD_DIM + 16
VMEM_LIMIT = 56 * 1024 * 1024

LOG2_E = 1.4426950408889634
SAFE_SCORE_BOUND = 48.0
BOUND_MARGIN = 1.02

TM_PROJ = 1024
TQ = 256
TK = 256
TILES_PER_ITER = 4
TM_FFN = 512
FF_CHUNK = 256

F32 = jnp.float32
BF16 = jnp.bfloat16


def _rms(x):
    return lax.rsqrt(jnp.mean(x * x, axis=-1, keepdims=True) + EPS)


def _inproj_kernel(x_ref, w_ref, hg_ref, e_ref, cos_ref, sin_ref,
                   qt_ref, k_ref, vt_ref, u_ref):
    x = x_ref[...]
    h = x.astype(BF16)
    r = _rms(x)
    e = e_ref[...]
    cos = cos_ref[...]
    sin = sin_ref[...]
    lane = lax.broadcasted_iota(jnp.int32, cos.shape, 1)
    first_half = (lane & ROPE_AXIS_PAIRS) == 0
    tq = qt_ref.shape[-1]

    def project(c0, width):
        return jnp.dot(h, w_ref[:, c0:c0 + width], preferred_element_type=F32) * r

    def head_norm_rope(p, c0):
        width = p.shape[1]
        ss = jnp.dot((p * p).astype(BF16), e[:width, :width], preferred_element_type=F32)
        pn = p * lax.rsqrt(ss * (1.0 / HEAD_DIM) + EPS) * hg_ref[:, c0:c0 + width]
        out = []
        for c in range(width // LANES):
            xc = pn[:, c * LANES:(c + 1) * LANES]
            partner = jnp.where(first_half,
                                pltpu.roll(xc, LANES - ROPE_AXIS_PAIRS, 1),
                                pltpu.roll(xc, ROPE_AXIS_PAIRS, 1))
            out.append(xc * cos + partner * sin)
        return out

    def q_epilogue(p, blk):
        roped = head_norm_rope(p, blk * MXU_DIM)
        for c, rc in enumerate(roped):
            qt = rc.T.astype(BF16)
            for t in range(qt_ref.shape[2]):
                qt_ref[0, blk * (MXU_DIM // LANES) + c, t] = qt[:, t * tq:(t + 1) * tq]

    def kv_epilogue(kv):
        kc, = head_norm_rope(kv[:, :D_KV], D_ATTN)
        k_ref[:, 0:LANES] = kc.astype(BF16)
        k_ref[:, LANES:2 * LANES] = pltpu.roll(kc, HEAD_DIM, 1).astype(BF16)
        vt = kv[:, D_KV:].T
        ones = jnp.ones((VT_ROWS - HEAD_DIM, vt.shape[1]), BF16)
        for g in range(N_KV_HEADS):
            vt_ref[0, g, 0:HEAD_DIM, :] = vt[g * HEAD_DIM:(g + 1) * HEAD_DIM].astype(BF16)
            vt_ref[0, g, HEAD_DIM:VT_ROWS, :] = ones

    p_q0 = project(0, MXU_DIM)
    p_q1 = project(MXU_DIM, MXU_DIM)
    q_epilogue(p_q0, 0)
    p_kv = project(D_ATTN, 2 * D_KV)
    q_epilogue(p_q1, 1)
    u_ref[:, 0:MXU_DIM] = project(D_QK + D_KV, MXU_DIM)
    kv_epilogue(p_kv)
    u_ref[:, MXU_DIM:] = project(D_QK + D_KV + MXU_DIM, MXU_DIM)


def _inproj(x2d, w, hg, e, cos, sin, n_batch, tm, tq):
    n_rows = x2d.shape[0]
    tiles = n_rows // tm
    tpb = tiles // n_batch
    n_pos = cos.shape[0] // tm
    qpt = tm // tq
    n_pairs = N_HEADS // 2
    const = lambda i: (0, 0)
    return pl.pallas_call(
        _inproj_kernel,
        grid=(tiles,),
        in_specs=[
            pl.BlockSpec((tm, D_MODEL), lambda i: (i, 0)),
            pl.BlockSpec((D_MODEL, D_IN), const),
            pl.BlockSpec((1, D_QK), const),
            pl.BlockSpec((MXU_DIM, MXU_DIM), const),
            pl.BlockSpec((tm, LANES), lambda i: (i % n_pos, 0)),
            pl.BlockSpec((tm, LANES), lambda i: (i % n_pos, 0)),
        ],
        out_specs=[
            pl.BlockSpec((1, n_pairs, qpt, LANES, tq), lambda i: (i // tpb, 0, i % tpb, 0, 0)),
            pl.BlockSpec((tm, 2 * D_KV), lambda i: (i, 0)),
            pl.BlockSpec((1, N_KV_HEADS, VT_ROWS, tm), lambda i: (i // tpb, 0, 0, i % tpb)),
            pl.BlockSpec((tm, D_POOL), lambda i: (i, 0)),
        ],
        out_shape=[
            jax.ShapeDtypeStruct((n_batch, n_pairs, tpb * qpt, LANES, tq), BF16),
            jax.ShapeDtypeStruct((n_rows, 2 * D_KV), BF16),
            jax.ShapeDtypeStruct((n_batch, N_KV_HEADS, VT_ROWS, tpb * tm), BF16),
            jax.ShapeDtypeStruct((n_rows, D_POOL), F32),
        ],
        compiler_params=pltpu.CompilerParams(
            dimension_semantics=("parallel",), vmem_limit_bytes=VMEM_LIMIT),
        name="inproj",
    )(x2d, w, hg, e, cos, sin)


def _attn_kernel(bound_ref, q_ref, k_ref, vt_ref, kmeta_ref, vtmeta_ref,
                 wo_ref, gmix_ref, wi_ref, gffn_ref, wd_ref,
                 o_ref, wo_out, wi_out, wd_out, kx_ref, vx_ref, s_ref,
                 *, tq, tk, seq, bounded):
    wo_out[...] = (wo_ref[...] * gmix_ref[...]).astype(BF16)
    wi_out[...] = (wi_ref[...] * gffn_ref[...]).astype(BF16)
    wd_out[...] = wd_ref[...].astype(BF16)

    n_qt = seq // tq
    nq = 2 * tq
    n_chunks = seq // tk
    n_keys = seq + N_META
    bounds = [(c * tk, (c + 1) * tk) for c in range(n_chunks)]
    bounds[-1] = (bounds[-1][0], n_keys)

    last_lo = bounds[-1][0]
    kx_ref[0:seq - last_lo, :] = k_ref[last_lo:seq, :]
    kx_ref[seq - last_lo:n_keys - last_lo, :] = kmeta_ref[...]
    vx_ref[:, 0:seq - last_lo] = vt_ref[0, 0, :, last_lo:seq]
    vx_ref[:, seq - last_lo:seq - last_lo + LANES] = vtmeta_ref[0]

    def keys(c):
        lo, hi = bounds[c]
        return kx_ref[0:hi - lo, :] if c == n_chunks - 1 else k_ref[lo:hi, :]

    def values_t(c):
        lo, hi = bounds[c]
        return vx_ref[...] if c == n_chunks - 1 else vt_ref[0, 0, :, lo:hi]

    def weights(j, pair):
        qt = q_ref[0, pair, j]
        z = jnp.zeros((HEAD_DIM, tq), BF16)
        return jnp.concatenate([jnp.concatenate([qt[0:HEAD_DIM], z], axis=0),
                                jnp.concatenate([qt[HEAD_DIM:], z], axis=0)], axis=1)

    def score_chunk(w, slot, c, mx):
        lo, hi = bounds[c]
        s = jnp.dot(keys(c), w, preferred_element_type=F32)
        if bounded:
            s_ref[slot, lo:hi, :] = jnp.exp2(s - bound_ref[...]).astype(BF16)
            return mx
        s_ref[slot, lo:hi, :] = s
        return jnp.maximum(mx, jnp.max(s, axis=0, keepdims=True))

    def pv_chunk(slot, c, m, acc):
        lo, hi = bounds[c]
        if bounded:
            p = s_ref[slot, lo:hi, :]
        else:
            p = jnp.exp2(s_ref[slot, lo:hi, :] - m).astype(BF16)
        vt_c = values_t(c)
        if p.shape[0] < vt_c.shape[1]:
            p = jnp.concatenate([p, jnp.zeros((vt_c.shape[1] - p.shape[0], nq), BF16)], axis=0)
        return acc + jnp.dot(vt_c, p, preferred_element_type=F32)

    def stage(w_next, slot_next, slot_cur, m_cur):
        mx = jnp.full((1, nq), -jnp.inf, F32)
        acc = jnp.zeros((VT_ROWS, nq), F32)
        for c in range(n_chunks):
            mx = score_chunk(w_next, slot_next, c, mx)
            acc = pv_chunk(slot_cur, c, m_cur, acc)
        return mx, acc

    def finish(acc, j, pair):
        o = acc[0:HEAD_DIM] * (1.0 / acc[HEAD_DIM:HEAD_DIM + 1])
        both = jnp.concatenate([o[:, :tq], o[:, tq:]], axis=0)
        o_ref[pl.ds(pl.multiple_of(j * tq, tq), tq), pair * LANES:(pair + 1) * LANES] = both.T

    w0 = weights(0, 0)
    m0 = jnp.full((1, nq), -jnp.inf, F32)
    for c in range(n_chunks):
        m0 = score_chunk(w0, 0, c, m0)

    def body(jj, m_cur):
        for t in range(TILES_PER_ITER):
            j = jj * TILES_PER_ITER + t
            for pair in range(2):
                nxt = (j, 1) if pair == 0 else (jnp.minimum(j + 1, n_qt - 1), 0)
                m_cur, acc = stage(weights(*nxt), 1 - pair, pair, m_cur)
                finish(acc, j, pair)
        return m_cur

    lax.fori_loop(0, n_qt // TILES_PER_ITER, body, m0)


def _attention(bound, qt, k2, vt, kmeta2, vtmeta, w_out, g_mix, w_ffn_in, g_ffn, w_ffn_down,
               n_batch, tq, tk, bounded):
    n_rows = k2.shape[0]
    seq = n_rows // n_batch
    n_qt = seq // tq
    gw = KV_GROUP * HEAD_DIM
    n_steps = n_batch * N_KV_HEADS
    kernel = functools.partial(_attn_kernel, tq=tq, tk=tk, seq=seq, bounded=bounded)

    def slab(w):
        rows = w.shape[0] // n_steps
        assert rows * n_steps == w.shape[0] and rows % 16 == 0, w.shape
        return pl.BlockSpec((rows, w.shape[1]), lambda b, g: (b * N_KV_HEADS + g, 0))

    weights_in = (w_out, g_mix, w_ffn_in, g_ffn, w_ffn_down)
    weights_out = (w_out, w_ffn_in, w_ffn_down)
    return pl.pallas_call(
        kernel,
        grid=(n_batch, N_KV_HEADS),
        in_specs=[
            pl.BlockSpec((1, 1), lambda b, g: (0, 0)),
            pl.BlockSpec((1, KV_GROUP // 2, n_qt, LANES, tq), lambda b, g: (b, g, 0, 0, 0)),
            pl.BlockSpec((seq, LANES), lambda b, g: (b, g)),
            pl.BlockSpec((1, 1, VT_ROWS, seq), lambda b, g: (b, g, 0, 0)),
            pl.BlockSpec((N_META, LANES), lambda b, g: (0, g)),
            pl.BlockSpec((1, VT_ROWS, LANES), lambda b, g: (g, 0, 0)),
        ] + [slab(w) for w in weights_in],
        out_specs=[pl.BlockSpec((seq, gw), lambda b, g: (b, g))] + [slab(w) for w in weights_out],
        out_shape=[jax.ShapeDtypeStruct((n_rows, D_ATTN), F32)]
        + [jax.ShapeDtypeStruct(w.shape, BF16) for w in weights_out],
        scratch_shapes=[pltpu.VMEM((tk + N_META, LANES), BF16),
                        pltpu.VMEM((VT_ROWS, tk + LANES), BF16),
                        pltpu.VMEM((2, seq + N_META, 2 * tq), BF16 if bounded else F32)],
        compiler_params=pltpu.CompilerParams(
            dimension_semantics=("parallel", "parallel"), vmem_limit_bytes=VMEM_LIMIT),
        name="attention",
    )(bound, qt, k2, vt, kmeta2, vtmeta, *weights_in)


def _mix_ffn_kernel(x_ref, a_ref, u_ref, uprev_ref, unext_ref, umeta_ref,
                    wp_ref, wo_ref, wi_ref, wd_ref, gl_ref,
                    o_ref, ext_ref, act_ref, *, tm, tpb, seq_total):
    j = pl.program_id(0) % tpb

    a = a_ref[...]
    proj_a = jnp.dot(a.astype(BF16), wo_ref[0:D_ATTN, :], preferred_element_type=F32) * _rms(a)

    ext_ref[POOL_HALO:POOL_HALO + tm, :] = u_ref[...]
    ext_ref[0:POOL_HALO, :] = jnp.where(j == 0, umeta_ref[N_META - POOL_HALO:N_META, :],
                                        uprev_ref[...])
    ext_ref[POOL_HALO + tm:2 * POOL_HALO + tm, :] = jnp.where(j == tpb - 1, 0.0, unext_ref[...])
    ext_ref[2 * POOL_HALO + tm:, :] = jnp.zeros((EXT_SLACK, D_POOL), F32)

    t8 = j * tm + (tm - POOL_HALO + N_META) + lax.broadcasted_iota(
        jnp.int32, (POOL_HALO, POOL_GROUP), 0)
    diffs = []
    for gi, w in enumerate(POOL_WINDOWS):
        half = w // 2
        xg = ext_ref[:, gi * POOL_GROUP:(gi + 1) * POOL_GROUP]
        s, k, n = xg, 1, tm + 2 * POOL_HALO + EXT_SLACK
        while k < half:
            n -= POOL_HALO
            s = s[0:n] + s[k:k + n]
            k *= 2
        win = s[POOL_HALO - half:POOL_HALO - half + tm] + s[POOL_HALO:POOL_HALO + tm]
        u_self = xg[POOL_HALO:POOL_HALO + tm]
        cnt8 = jnp.minimum(t8 + half, seq_total) - (t8 - half)
        body = win[:tm - POOL_HALO] * (1.0 / w) - u_self[:tm - POOL_HALO]
        tail = win[tm - POOL_HALO:] * (1.0 / cnt8.astype(F32)) - u_self[tm - POOL_HALO:]
        diffs.append(jnp.concatenate([body, tail], axis=0))
    pooled = []
    for pr in range(2):
        d = jnp.concatenate(diffs[2 * pr:2 * pr + 2], axis=1).astype(BF16)
        pooled.append(jnp.dot(d, wp_ref[pr], preferred_element_type=F32))
    y = jnp.concatenate(pooled, axis=1)
    proj_p = jnp.dot(y.astype(BF16), wo_ref[D_ATTN:, :], preferred_element_type=F32) * _rms(y)
    h1 = x_ref[...] + proj_a + proj_p

    h1b = h1.astype(BF16)
    r1 = _rms(h1)
    for c in range(D_FF // FF_CHUNK):
        cols = slice(c * FF_CHUNK, (c + 1) * FF_CHUNK)
        gate = jnp.dot(h1b, wi_ref[:, cols], preferred_element_type=F32) * r1
        up = jnp.dot(h1b, wi_ref[:, D_FF + c * FF_CHUNK:D_FF + (c + 1) * FF_CHUNK],
                     preferred_element_type=F32) * r1
        act_ref[:, cols] = (gate * jax.nn.sigmoid(gate) * up).astype(BF16)
    h3 = h1 + jnp.dot(act_ref[...], wd_ref[...], preferred_element_type=F32)
    o_ref[...] = h3 * _rms(h3) * gl_ref[...]


def _mix_ffn(x2d, attn, u, umeta, wp, wo, wi, wd, gl, n_batch, tm):
    n_rows = x2d.shape[0]
    tiles = n_rows // tm
    tpb = tiles // n_batch
    hb = tm // POOL_HALO
    n_hblocks = n_rows // POOL_HALO
    kernel = functools.partial(_mix_ffn_kernel, tm=tm, tpb=tpb,
                               seq_total=N_META + n_rows // n_batch)

    def const(shape):
        zeros = (0,) * len(shape)
        return pl.BlockSpec(shape, lambda i: zeros, pipeline_mode=pl.Buffered(1))

    return pl.pallas_call(
        kernel,
        grid=(tiles,),
        in_specs=[
            pl.BlockSpec((tm, D_MODEL), lambda i: (i, 0)),
            pl.BlockSpec((tm, D_ATTN), lambda i: (i, 0)),
            pl.BlockSpec((tm, D_POOL), lambda i: (i, 0)),
            pl.BlockSpec((POOL_HALO, D_POOL), lambda i: (jnp.maximum(i * hb - 1, 0), 0)),
            pl.BlockSpec((POOL_HALO, D_POOL),
                         lambda i: (jnp.minimum((i + 1) * hb, n_hblocks - 1), 0)),
            const((N_META, D_POOL)),
            const((2, MXU_DIM, MXU_DIM)),
            const((D_MODEL, D_MODEL)),
            const((D_MODEL, 2 * D_FF)),
            const((D_FF, D_MODEL)),
            const((1, D_MODEL)),
        ],
        out_specs=pl.BlockSpec((tm, D_MODEL), lambda i: (i, 0)),
        out_shape=jax.ShapeDtypeStruct((n_rows, D_MODEL), F32),
        scratch_shapes=[pltpu.VMEM((tm + 2 * POOL_HALO + EXT_SLACK, D_POOL), F32),
                        pltpu.VMEM((tm, D_FF), BF16)],
        compiler_params=pltpu.CompilerParams(
            dimension_semantics=("parallel",), vmem_limit_bytes=VMEM_LIMIT),
        name="mix_ffn",
    )(x2d, attn, u, u, u, umeta, wp, wo, wi, wd, gl)


def _rope_tables(n_real):
    rows = n_real // GRID_W
    r = np.arange(n_real)
    row = (r // GRID_W - rows // 2).astype(np.float64)
    col = (r % GRID_W - GRID_W // 2).astype(np.float64)
    freqs = ROPE_THETA ** (-np.arange(ROPE_AXIS_PAIRS, dtype=np.float64) / ROPE_AXIS_PAIRS)
    ar = row[:, None] * freqs
    ac = col[:, None] * freqs
    cos = np.concatenate([np.cos(ar), np.cos(ar), np.cos(ac), np.cos(ac)], axis=1)
    sin = np.concatenate([-np.sin(ar), np.sin(ar), -np.sin(ac), np.sin(ac)], axis=1)
    return (jnp.asarray(np.tile(cos, (1, 2)), dtype=F32),
            jnp.asarray(np.tile(sin, (1, 2)), dtype=F32))


def kernel(x, meta_tokens, norm_mix, w_in, q_norm, k_norm, attn_out_norm, w_pool,
           pool_scale, w_out, norm_ffn, w_ffn_in, w_ffn_down, norm_final):
    n_batch, n_real, _ = x.shape
    assert n_real % TM_PROJ == 0 and n_real % TQ == 0 and n_real % TM_FFN == 0
    assert w_in.shape[0] == 1, "single layer"
    x2d = x.reshape(n_batch * n_real, D_MODEL)

    w_in_b = (norm_mix[0][:, None] * w_in[0]).astype(BF16)
    head_gain = jnp.concatenate([jnp.tile(q_norm[0], N_HEADS) * (HEAD_DIM ** -0.5 * LOG2_E),
                                 jnp.tile(k_norm[0], N_KV_HEADS)]).reshape(1, D_QK)
    blk = np.arange(MXU_DIM) // HEAD_DIM
    e_heads = jnp.asarray(blk[:, None] == blk[None, :], dtype=BF16)
    cos, sin = _rope_tables(n_real)

    wp = jnp.zeros((2, MXU_DIM, MXU_DIM), F32)
    for gi in range(len(POOL_WINDOWS)):
        r0 = (gi % 2) * POOL_GROUP
        wp = wp.at[gi // 2, r0:r0 + POOL_GROUP, r0:r0 + POOL_GROUP].set(w_pool[0, gi])
    wp = wp.astype(BF16)
    g_mix = jnp.concatenate([attn_out_norm[0], pool_scale[0]]).reshape(D_MODEL, 1)
    g_ffn = norm_ffn[0].reshape(D_MODEL, 1)

    qt, k2, vt, u = _inproj(x2d, w_in_b, head_gain, e_heads, cos, sin, n_batch, TM_PROJ, TQ)
    meta_pad = jnp.zeros((TQ, D_MODEL), F32).at[:N_META].set(meta_tokens)
    ident_cos = jnp.asarray(np.ones((TQ, LANES), np.float32))
    ident_sin = jnp.asarray(np.zeros((TQ, LANES), np.float32))
    _, kmeta2, vtmeta, umeta = _inproj(meta_pad, w_in_b, head_gain, e_heads,
                                       ident_cos, ident_sin, 1, TQ, TQ)
    kmeta2 = kmeta2[:N_META]
    lane = jnp.arange(LANES)
    vtmeta = jnp.where(lane[None, None, :] < N_META, vtmeta[0, :, :, :LANES], 0).astype(BF16)
    umeta = umeta[:N_META]

    bound = (HEAD_DIM * BOUND_MARGIN * jnp.max(jnp.abs(head_gain[:, :D_ATTN]))
             * jnp.max(jnp.abs(head_gain[:, D_ATTN:]))).reshape(1, 1)
    attn_args = (bound, qt, k2, vt, kmeta2, vtmeta,
                 w_out[0], g_mix, w_ffn_in[0], g_ffn, w_ffn_down[0])
    attn, w_out_b, wi, wd = lax.cond(
        bound[0, 0] <= SAFE_SCORE_BOUND,
        lambda args: _attention(*args, n_batch, TQ, TK, bounded=True),
        lambda args: _attention(*args, n_batch, TQ, TK, bounded=False),
        attn_args)

    out = _mix_ffn(x2d, attn, u, umeta, wp, w_out_b, wi, wd,
                   norm_final.reshape(1, D_MODEL), n_batch, TM_FFN)
    return out.reshape(n_batch, n_real, D_MODEL)
```

```python
import functools

import jax
import jax.numpy as jnp
import numpy as np
from jax import lax
from jax.experimental import pallas as pl
from jax.experimental.pallas import tpu as pltpu

D_MODEL = 1024
N_META = 16
GRID_W = 64
N_HEADS = 8
N_KV_HEADS = 2
HEAD_DIM = 64
KV_GROUP = N_HEADS // N_KV_HEADS
D_ATTN = N_HEADS * HEAD_DIM
D_KV = N_KV_HEADS * HEAD_DIM
D_QK = D_ATTN + D_KV
D_POOL = 512
POOL_WINDOWS = (2, 4, 8, 16)
POOL_GROUP = 128
POOL_HALO = 8
EXT_SLACK = 16
D_IN = D_ATTN + 2 * D_KV + D_POOL
D_FF = 2816
ROPE_AXIS_PAIRS = HEAD_DIM // 4
ROPE_THETA = 10000.0
EPS = 1e-6

LANES = 128
MXU_DIM = 256
VT_ROWS = HEAD_DIM + 16
VMEM_LIMIT = 56 * 1024 * 1024

LOG2_E = 1.4426950408889634

TM_PROJ = 1024
TQ = 256
TK = 256
TILES_PER_ITER = 4
TM_FFN = 512
FF_CHUNK = 256

F32 = jnp.float32
BF16 = jnp.bfloat16


def _rms(x):
    return lax.rsqrt(jnp.mean(x * x, axis=-1, keepdims=True) + EPS)


def _inproj_kernel(x_ref, w_ref, hg_ref, e_ref, cos_ref, sin_ref,
                   q_ref, k_ref, vt_ref, u_ref):
    x = x_ref[...]
    h = x.astype(BF16)
    r = _rms(x)
    e = e_ref[...]
    cos = cos_ref[...]
    sin = sin_ref[...]
    lane = lax.broadcasted_iota(jnp.int32, cos.shape, 1)
    first_half = (lane & ROPE_AXIS_PAIRS) == 0

    def project(c0, width):
        return jnp.dot(h, w_ref[:, c0:c0 + width], preferred_element_type=F32) * r

    def head_norm_rope(p, c0):
        width = p.shape[1]
        ss = jnp.dot((p * p).astype(BF16), e[:width, :width], preferred_element_type=F32)
        pn = p * lax.rsqrt(ss * (1.0 / HEAD_DIM) + EPS) * hg_ref[:, c0:c0 + width]
        out = []
        for c in range(width // LANES):
            xc = pn[:, c * LANES:(c + 1) * LANES]
            partner = jnp.where(first_half,
                                pltpu.roll(xc, LANES - ROPE_AXIS_PAIRS, 1),
                                pltpu.roll(xc, ROPE_AXIS_PAIRS, 1))
            out.append(xc * cos + partner * sin)
        return out

    def q_epilogue(p, blk):
        roped = head_norm_rope(p, blk * MXU_DIM)
        for c, rc in enumerate(roped):
            c0 = blk * MXU_DIM + c * LANES
            q_ref[:, c0:c0 + LANES] = rc.astype(BF16)

    def kv_epilogue(kv):
        kc, = head_norm_rope(kv[:, :D_KV], D_ATTN)
        k_ref[:, 0:LANES] = kc.astype(BF16)
        k_ref[:, LANES:2 * LANES] = pltpu.roll(kc, HEAD_DIM, 1).astype(BF16)
        vt = kv[:, D_KV:].T
        ones = jnp.ones((VT_ROWS - HEAD_DIM, vt.shape[1]), BF16)
        for g in range(N_KV_HEADS):
            vt_ref[0, g, 0:HEAD_DIM, :] = vt[g * HEAD_DIM:(g + 1) * HEAD_DIM].astype(BF16)
            vt_ref[0, g, HEAD_DIM:VT_ROWS, :] = ones

    p_q0 = project(0, MXU_DIM)
    p_q1 = project(MXU_DIM, MXU_DIM)
    q_epilogue(p_q0, 0)
    p_kv = project(D_ATTN, 2 * D_KV)
    q_epilogue(p_q1, 1)
    u_ref[:, 0:MXU_DIM] = project(D_QK + D_KV, MXU_DIM)
    kv_epilogue(p_kv)
    u_ref[:, MXU_DIM:] = project(D_QK + D_KV + MXU_DIM, MXU_DIM)


def _inproj(x2d, w, hg, e, cos, sin, n_batch, tm):
    n_rows = x2d.shape[0]
    tiles = n_rows // tm
    tpb = tiles // n_batch
    n_pos = cos.shape[0] // tm
    const = lambda i: (0, 0)
    return pl.pallas_call(
        _inproj_kernel,
        grid=(tiles,),
        in_specs=[
            pl.BlockSpec((tm, D_MODEL), lambda i: (i, 0)),
            pl.BlockSpec((D_MODEL, D_IN), const),
            pl.BlockSpec((1, D_QK), const),
            pl.BlockSpec((MXU_DIM, MXU_DIM), const),
            pl.BlockSpec((tm, LANES), lambda i: (i % n_pos, 0)),
            pl.BlockSpec((tm, LANES), lambda i: (i % n_pos, 0)),
        ],
        out_specs=[
            pl.BlockSpec((tm, D_ATTN), lambda i: (i, 0)),
            pl.BlockSpec((tm, 2 * D_KV), lambda i: (i, 0)),
            pl.BlockSpec((1, N_KV_HEADS, VT_ROWS, tm), lambda i: (i // tpb, 0, 0, i % tpb)),
            pl.BlockSpec((tm, D_POOL), lambda i: (i, 0)),
        ],
        out_shape=[
            jax.ShapeDtypeStruct((n_rows, D_ATTN), BF16),
            jax.ShapeDtypeStruct((n_rows, 2 * D_KV), BF16),
            jax.ShapeDtypeStruct((n_batch, N_KV_HEADS, VT_ROWS, tpb * tm), BF16),
            jax.ShapeDtypeStruct((n_rows, D_POOL), F32),
        ],
        compiler_params=pltpu.CompilerParams(
            dimension_semantics=("parallel",), vmem_limit_bytes=VMEM_LIMIT),
        name="inproj",
    )(x2d, w, hg, e, cos, sin)


def _attn_kernel(q_ref, k_ref, vt_ref, kmeta_ref, vtmeta_ref,
                 wo_ref, gmix_ref, wi_ref, gffn_ref, wd_ref,
                 o_ref, wo_out, wi_out, wd_out, kx_ref, vx_ref, s_ref,
                 *, tq, tk, seq):
    wo_out[...] = (wo_ref[...] * gmix_ref[...]).astype(BF16)
    wi_out[...] = (wi_ref[...] * gffn_ref[...]).astype(BF16)
    wd_out[...] = wd_ref[...].astype(BF16)

    n_qt = seq // tq
    nq = 2 * tq
    n_chunks = seq // tk
    n_keys = seq + N_META
    bounds = [(c * tk, (c + 1) * tk) for c in range(n_chunks)]
    bounds[-1] = (bounds[-1][0], n_keys)

    last_lo = bounds[-1][0]
    kx_ref[0:seq - last_lo, :] = k_ref[last_lo:seq, :]
    kx_ref[seq - last_lo:n_keys - last_lo, :] = kmeta_ref[...]
    vx_ref[:, 0:seq - last_lo] = vt_ref[0, 0, :, last_lo:seq]
    vx_ref[:, seq - last_lo:seq - last_lo + LANES] = vtmeta_ref[0]

    def keys(c):
        lo, hi = bounds[c]
        return kx_ref[0:hi - lo, :] if c == n_chunks - 1 else k_ref[lo:hi, :]

    def values_t(c):
        lo, hi = bounds[c]
        return vx_ref[...] if c == n_chunks - 1 else vt_ref[0, 0, :, lo:hi]

    def weights(j, pair):
        rows = pl.ds(pl.multiple_of(j * tq, tq), tq)
        qt = q_ref[rows, pair * LANES:(pair + 1) * LANES].astype(F32).T.astype(BF16)
        z = jnp.zeros((HEAD_DIM, tq), BF16)
        return jnp.concatenate([jnp.concatenate([qt[0:HEAD_DIM], z], axis=0),
                                jnp.concatenate([qt[HEAD_DIM:], z], axis=0)], axis=1)

    def score_chunk(w, slot, c, mx):
        lo, hi = bounds[c]
        s = jnp.dot(keys(c), w, preferred_element_type=F32)
        s_ref[slot, lo:hi, :] = s
        return jnp.maximum(mx, jnp.max(s, axis=0, keepdims=True))

    def pv_chunk(slot, c, m, acc):
        lo, hi = bounds[c]
        p = jnp.exp2(s_ref[slot, lo:hi, :] - m).astype(BF16)
        vt_c = values_t(c)
        if p.shape[0] < vt_c.shape[1]:
            p = jnp.concatenate([p, jnp.zeros((vt_c.shape[1] - p.shape[0], nq), BF16)], axis=0)
        return acc + jnp.dot(vt_c, p, preferred_element_type=F32)

    def stage(w_next, slot_next, slot_cur, m_cur):
        mx = jnp.full((1, nq), -jnp.inf, F32)
        acc = jnp.zeros((VT_ROWS, nq), F32)
        for c in range(n_chunks):
            mx = score_chunk(w_next, slot_next, c, mx)
            acc = pv_chunk(slot_cur, c, m_cur, acc)
        return mx, acc

    def finish(acc, j, pair):
        o = acc[0:HEAD_DIM] * (1.0 / acc[HEAD_DIM:HEAD_DIM + 1])
        both = jnp.concatenate([o[:, :tq], o[:, tq:]], axis=0)
        o_ref[pl.ds(pl.multiple_of(j * tq, tq), tq), pair * LANES:(pair + 1) * LANES] = both.T

    w0 = weights(0, 0)
    m0 = jnp.full((1, nq), -jnp.inf, F32)
    for c in range(n_chunks):
        m0 = score_chunk(w0, 0, c, m0)

    def body(jj, m_cur):
        for t in range(TILES_PER_ITER):
            j = jj * TILES_PER_ITER + t
            for pair in range(2):
                nxt = (j, 1) if pair == 0 else (jnp.minimum(j + 1, n_qt - 1), 0)
                m_cur, acc = stage(weights(*nxt), 1 - pair, pair, m_cur)
                finish(acc, j, pair)
        return m_cur

    lax.fori_loop(0, n_qt // TILES_PER_ITER, body, m0)


def _attention(qt, k2, vt, kmeta2, vtmeta, w_out, g_mix, w_ffn_in, g_ffn, w_ffn_down,
               n_batch, tq, tk):
    n_rows = k2.shape[0]
    seq = n_rows // n_batch
    n_qt = seq // tq
    gw = KV_GROUP * HEAD_DIM
    n_steps = n_batch * N_KV_HEADS
    kernel = functools.partial(_attn_kernel, tq=tq, tk=tk, seq=seq)

    def slab(w):
        rows = w.shape[0] // n_steps
        assert rows * n_steps == w.shape[0] and rows % 16 == 0, w.shape
        return pl.BlockSpec((rows, w.shape[1]), lambda b, g: (b * N_KV_HEADS + g, 0))

    weights_in = (w_out, g_mix, w_ffn_in, g_ffn, w_ffn_down)
    weights_out = (w_out, w_ffn_in, w_ffn_down)
    return pl.pallas_call(
        kernel,
        grid=(n_batch, N_KV_HEADS),
        in_specs=[
            pl.BlockSpec((seq, gw), lambda b, g: (b, g)),
            pl.BlockSpec((seq, LANES), lambda b, g: (b, g)),
            pl.BlockSpec((1, 1, VT_ROWS, seq), lambda b, g: (b, g, 0, 0)),
            pl.BlockSpec((N_META, LANES), lambda b, g: (0, g)),
            pl.BlockSpec((1, VT_ROWS, LANES), lambda b, g: (g, 0, 0)),
        ] + [slab(w) for w in weights_in],
        out_specs=[pl.BlockSpec((seq, gw), lambda b, g: (b, g))] + [slab(w) for w in weights_out],
        out_shape=[jax.ShapeDtypeStruct((n_rows, D_ATTN), F32)]
        + [jax.ShapeDtypeStruct(w.shape, BF16) for w in weights_out],
        scratch_shapes=[pltpu.VMEM((tk + N_META, LANES), BF16),
                        pltpu.VMEM((VT_ROWS, tk + LANES), BF16),
                        pltpu.VMEM((2, seq + N_META, 2 * tq), F32)],
        compiler_params=pltpu.CompilerParams(
            dimension_semantics=("parallel", "parallel"), vmem_limit_bytes=VMEM_LIMIT),
        name="attention",
    )(qt, k2, vt, kmeta2, vtmeta, *weights_in)


def _mix_ffn_kernel(x_ref, a_ref, u_ref, uprev_ref, unext_ref, umeta_ref,
                    wp_ref, wo_ref, wi_ref, wd_ref, gl_ref,
                    o_ref, ext_ref, act_ref, *, tm, tpb, seq_total):
    j = pl.program_id(0) % tpb

    a = a_ref[...]
    proj_a = jnp.dot(a.astype(BF16), wo_ref[0:D_ATTN, :], preferred_element_type=F32) * _rms(a)

    ext_ref[POOL_HALO:POOL_HALO + tm, :] = u_ref[...]
    ext_ref[0:POOL_HALO, :] = jnp.where(j == 0, umeta_ref[N_META - POOL_HALO:N_META, :],
                                        uprev_ref[...])
    ext_ref[POOL_HALO + tm:2 * POOL_HALO + tm, :] = jnp.where(j == tpb - 1, 0.0, unext_ref[...])
    ext_ref[2 * POOL_HALO + tm:, :] = jnp.zeros((EXT_SLACK, D_POOL), F32)

    t8 = j * tm + (tm - POOL_HALO + N_META) + lax.broadcasted_iota(
        jnp.int32, (POOL_HALO, POOL_GROUP), 0)
    diffs = []
    for gi, w in enumerate(POOL_WINDOWS):
        half = w // 2
        xg = ext_ref[:, gi * POOL_GROUP:(gi + 1) * POOL_GROUP]
        s, k, n = xg, 1, tm + 2 * POOL_HALO + EXT_SLACK
        while k < half:
            n -= POOL_HALO
            s = s[0:n] + s[k:k + n]
            k *= 2
        win = s[POOL_HALO - half:POOL_HALO - half + tm] + s[POOL_HALO:POOL_HALO + tm]
        u_self = xg[POOL_HALO:POOL_HALO + tm]
        cnt8 = jnp.minimum(t8 + half, seq_total) - (t8 - half)
        body = win[:tm - POOL_HALO] * (1.0 / w) - u_self[:tm - POOL_HALO]
        tail = win[tm - POOL_HALO:] * (1.0 / cnt8.astype(F32)) - u_self[tm - POOL_HALO:]
        diffs.append(jnp.concatenate([body, tail], axis=0))
    pooled = []
    for pr in range(2):
        d = jnp.concatenate(diffs[2 * pr:2 * pr + 2], axis=1).astype(BF16)
        pooled.append(jnp.dot(d, wp_ref[pr], preferred_element_type=F32))
    y = jnp.concatenate(pooled, axis=1)
    proj_p = jnp.dot(y.astype(BF16), wo_ref[D_ATTN:, :], preferred_element_type=F32) * _rms(y)
    h1 = x_ref[...] + proj_a + proj_p

    h1b = h1.astype(BF16)
    r1 = _rms(h1)
    for c in range(D_FF // FF_CHUNK):
        cols = slice(c * FF_CHUNK, (c + 1) * FF_CHUNK)
        gate = jnp.dot(h1b, wi_ref[:, cols], preferred_element_type=F32) * r1
        up = jnp.dot(h1b, wi_ref[:, D_FF + c * FF_CHUNK:D_FF + (c + 1) * FF_CHUNK],
                     preferred_element_type=F32) * r1
        act_ref[:, cols] = (gate * jax.nn.sigmoid(gate) * up).astype(BF16)
    h3 = h1 + jnp.dot(act_ref[...], wd_ref[...], preferred_element_type=F32)
    o_ref[...] = h3 * _rms(h3) * gl_ref[...]


def _mix_ffn(x2d, attn, u, umeta, wp, wo, wi, wd, gl, n_batch, tm):
    n_rows = x2d.shape[0]
    tiles = n_rows // tm
    tpb = tiles // n_batch
    hb = tm // POOL_HALO
    n_hblocks = n_rows // POOL_HALO
    kernel = functools.partial(_mix_ffn_kernel, tm=tm, tpb=tpb,
                               seq_total=N_META + n_rows // n_batch)

    def const(shape):
        zeros = (0,) * len(shape)
        return pl.BlockSpec(shape, lambda i: zeros, pipeline_mode=pl.Buffered(1))

    return pl.pallas_call(
        kernel,
        grid=(tiles,),
        in_specs=[
            pl.BlockSpec((tm, D_MODEL), lambda i: (i, 0)),
            pl.BlockSpec((tm, D_ATTN), lambda i: (i, 0)),
            pl.BlockSpec((tm, D_POOL), lambda i: (i, 0)),
            pl.BlockSpec((POOL_HALO, D_POOL), lambda i: (jnp.maximum(i * hb - 1, 0), 0)),
            pl.BlockSpec((POOL_HALO, D_POOL),
                         lambda i: (jnp.minimum((i + 1) * hb, n_hblocks - 1), 0)),
            const((N_META, D_POOL)),
            const((2, MXU_DIM, MXU_DIM)),
            const((D_MODEL, D_MODEL)),
            const((D_MODEL, 2 * D_FF)),
            const((D_FF, D_MODEL)),
            const((1, D_MODEL)),
        ],
        out_specs=pl.BlockSpec((tm, D_MODEL), lambda i: (i, 0)),
        out_shape=jax.ShapeDtypeStruct((n_rows, D_MODEL), F32),
        scratch_shapes=[pltpu.VMEM((tm + 2 * POOL_HALO + EXT_SLACK, D_POOL), F32),
                        pltpu.VMEM((tm, D_FF), BF16)],
        compiler_params=pltpu.CompilerParams(
            dimension_semantics=("parallel",), vmem_limit_bytes=VMEM_LIMIT),
        name="mix_ffn",
    )(x2d, attn, u, u, u, umeta, wp, wo, wi, wd, gl)


def _rope_tables(n_real):
    rows = n_real // GRID_W
    r = np.arange(n_real)
    row = (r // GRID_W - rows // 2).astype(np.float64)
    col = (r % GRID_W - GRID_W // 2).astype(np.float64)
    freqs = ROPE_THETA ** (-np.arange(ROPE_AXIS_PAIRS, dtype=np.float64) / ROPE_AXIS_PAIRS)
    ar = row[:, None] * freqs
    ac = col[:, None] * freqs
    cos = np.concatenate([np.cos(ar), np.cos(ar), np.cos(ac), np.cos(ac)], axis=1)
    sin = np.concatenate([-np.sin(ar), np.sin(ar), -np.sin(ac), np.sin(ac)], axis=1)
    return (jnp.asarray(np.tile(cos, (1, 2)), dtype=F32),
            jnp.asarray(np.tile(sin, (1, 2)), dtype=F32))


def kernel(x, meta_tokens, norm_mix, w_in, q_norm, k_norm, attn_out_norm, w_pool,
           pool_scale, w_out, norm_ffn, w_ffn_in, w_ffn_down, norm_final):
    n_batch, n_real, _ = x.shape
    assert n_real % TM_PROJ == 0 and n_real % TQ == 0 and n_real % TM_FFN == 0
    assert w_in.shape[0] == 1, "single layer"
    x2d = x.reshape(n_batch * n_real, D_MODEL)

    w_in_b = (norm_mix[0][:, None] * w_in[0]).astype(BF16)
    head_gain = jnp.concatenate([jnp.tile(q_norm[0], N_HEADS) * (HEAD_DIM ** -0.5 * LOG2_E),
                                 jnp.tile(k_norm[0], N_KV_HEADS)]).reshape(1, D_QK)
    blk = np.arange(MXU_DIM) // HEAD_DIM
    e_heads = jnp.asarray(blk[:, None] == blk[None, :], dtype=BF16)
    cos, sin = _rope_tables(n_real)

    wp = jnp.zeros((2, MXU_DIM, MXU_DIM), F32)
    for gi in range(len(POOL_WINDOWS)):
        r0 = (gi % 2) * POOL_GROUP
        wp = wp.at[gi // 2, r0:r0 + POOL_GROUP, r0:r0 + POOL_GROUP].set(w_pool[0, gi])
    wp = wp.astype(BF16)
    g_mix = jnp.concatenate([attn_out_norm[0], pool_scale[0]]).reshape(D_MODEL, 1)
    g_ffn = norm_ffn[0].reshape(D_MODEL, 1)

    qt, k2, vt, u = _inproj(x2d, w_in_b, head_gain, e_heads, cos, sin, n_batch, TM_PROJ)
    meta_pad = jnp.zeros((TQ, D_MODEL), F32).at[:N_META].set(meta_tokens)
    ident_cos = jnp.asarray(np.ones((TQ, LANES), np.float32))
    ident_sin = jnp.asarray(np.zeros((TQ, LANES), np.float32))
    _, kmeta2, vtmeta, umeta = _inproj(meta_pad, w_in_b, head_gain, e_heads,
                                       ident_cos, ident_sin, 1, TQ)
    kmeta2 = kmeta2[:N_META]
    lane = jnp.arange(LANES)
    vtmeta = jnp.where(lane[None, None, :] < N_META, vtmeta[0, :, :, :LANES], 0).astype(BF16)
    umeta = umeta[:N_META]

    attn, w_out_b, wi, wd = _attention(qt, k2, vt, kmeta2, vtmeta,
                                       w_out[0], g_mix, w_ffn_in[0], g_ffn, w_ffn_down[0],
                                       n_batch, TQ, TK)

    out = _mix_ffn(x2d, attn, u, umeta, wp, w_out_b, wi, wd,
                   norm_final.reshape(1, D_MODEL), n_batch, TM_FFN)
    return out.reshape(n_batch, n_real, D_MODEL)
```

```python
import functools

import jax
import jax.numpy as jnp
import numpy as np
from jax import lax
from jax.experimental import pallas as pl
from jax.experimental.pallas import tpu as pltpu

D_MODEL = 1024
N_META = 16
GRID_W = 64
N_HEADS = 8
N_KV_HEADS = 2
HEAD_DIM = 64
KV_GROUP = N_HEADS // N_KV_HEADS
D_ATTN = N_HEADS * HEAD_DIM
D_KV = N_KV_HEADS * HEAD_DIM
D_QK = D_ATTN + D_KV
D_POOL = 512
POOL_WINDOWS = (2, 4, 8, 16)
POOL_GROUP = 128
POOL_HALO = 8
EXT_SLACK = 16
D_IN = D_ATTN + 2 * D_KV + D_POOL
D_FF = 2816
ROPE_AXIS_PAIRS = HEAD_DIM // 4
ROPE_THETA = 10000.0
EPS = 1e-6

LANES = 128
MXU_DIM = 256
VT_ROWS = HEAD_DIM + 16
VMEM_LIMIT = 56 * 1024 * 1024

LOG2_E = 1.4426950408889634

TM_PROJ = 1024
TQ = 256
TK = 256
TILES_PER_ITER = 4
TM_FFN = 512
FF_CHUNK = 256

F32 = jnp.float32
BF16 = jnp.bfloat16


def _rms(x):
    return lax.rsqrt(jnp.mean(x * x, axis=-1, keepdims=True) + EPS)


def _inproj_kernel(x_ref, w_ref, hg_ref, e_ref, cos_row_ref, sin_row_ref, cos_col_ref, sin_col_ref,
                   q_ref, k_ref, vt_ref, u_ref):
    x = x_ref[...]
    h = x.astype(BF16)
    r = _rms(x)
    e = e_ref[...]

    def table(row_ref, col_ref):
        col = col_ref[...]
        return jnp.concatenate([row_ref[g:g + 1, :] + col for g in range(row_ref.shape[0])],
                               axis=0)

    cos = table(cos_row_ref, cos_col_ref)
    sin = table(sin_row_ref, sin_col_ref)
    lane = lax.broadcasted_iota(jnp.int32, cos.shape, 1)
    first_half = (lane & ROPE_AXIS_PAIRS) == 0

    def project(c0, width):
        return jnp.dot(h, w_ref[:, c0:c0 + width], preferred_element_type=F32) * r

    def head_norm_rope(p, c0):
        width = p.shape[1]
        ss = jnp.dot((p * p).astype(BF16), e[:width, :width], preferred_element_type=F32)
        pn = p * lax.rsqrt(ss * (1.0 / HEAD_DIM) + EPS) * hg_ref[:, c0:c0 + width]
        out = []
        for c in range(width // LANES):
            xc = pn[:, c * LANES:(c + 1) * LANES]
            partner = jnp.where(first_half,
                                pltpu.roll(xc, LANES - ROPE_AXIS_PAIRS, 1),
                                pltpu.roll(xc, ROPE_AXIS_PAIRS, 1))
            out.append(xc * cos + partner * sin)
        return out

    def q_epilogue(p, blk):
        roped = head_norm_rope(p, blk * MXU_DIM)
        for c, rc in enumerate(roped):
            c0 = blk * MXU_DIM + c * LANES
            q_ref[:, c0:c0 + LANES] = rc.astype(BF16)

    def kv_epilogue(kv):
        kc, = head_norm_rope(kv[:, :D_KV], D_ATTN)
        k_ref[:, 0:LANES] = kc.astype(BF16)
        k_ref[:, LANES:2 * LANES] = pltpu.roll(kc, HEAD_DIM, 1).astype(BF16)
        vt = kv[:, D_KV:].T
        ones = jnp.ones((VT_ROWS - HEAD_DIM, vt.shape[1]), BF16)
        for g in range(N_KV_HEADS):
            vt_ref[0, g, 0:HEAD_DIM, :] = vt[g * HEAD_DIM:(g + 1) * HEAD_DIM].astype(BF16)
            vt_ref[0, g, HEAD_DIM:VT_ROWS, :] = ones

    p_q0 = project(0, MXU_DIM)
    p_q1 = project(MXU_DIM, MXU_DIM)
    q_epilogue(p_q0, 0)
    p_kv = project(D_ATTN, 2 * D_KV)
    q_epilogue(p_q1, 1)
    u_ref[:, 0:MXU_DIM] = project(D_QK + D_KV, MXU_DIM)
    kv_epilogue(p_kv)
    u_ref[:, MXU_DIM:] = project(D_QK + D_KV + MXU_DIM, MXU_DIM)


def _inproj(x2d, w, hg, e, rope, n_batch, tm):
    n_rows = x2d.shape[0]
    tiles = n_rows // tm
    tpb = tiles // n_batch
    assert tm % GRID_W == 0
    rpt = tm // GRID_W
    n_pos = rope[0].shape[0] // rpt
    const = lambda i: (0, 0)
    row_spec = pl.BlockSpec((rpt, LANES), lambda i: (i % n_pos, 0))
    col_spec = pl.BlockSpec((GRID_W, LANES), const)
    return pl.pallas_call(
        _inproj_kernel,
        grid=(tiles,),
        in_specs=[
            pl.BlockSpec((tm, D_MODEL), lambda i: (i, 0)),
            pl.BlockSpec((D_MODEL, D_IN), const),
            pl.BlockSpec((1, D_QK), const),
            pl.BlockSpec((MXU_DIM, MXU_DIM), const),
            row_spec, row_spec, col_spec, col_spec,
        ],
        out_specs=[
            pl.BlockSpec((tm, D_ATTN), lambda i: (i, 0)),
            pl.BlockSpec((tm, 2 * D_KV), lambda i: (i, 0)),
            pl.BlockSpec((1, N_KV_HEADS, VT_ROWS, tm), lambda i: (i // tpb, 0, 0, i % tpb)),
            pl.BlockSpec((tm, D_POOL), lambda i: (i, 0)),
        ],
        out_shape=[
            jax.ShapeDtypeStruct((n_rows, D_ATTN), BF16),
            jax.ShapeDtypeStruct((n_rows, 2 * D_KV), BF16),
            jax.ShapeDtypeStruct((n_batch, N_KV_HEADS, VT_ROWS, tpb * tm), BF16),
            jax.ShapeDtypeStruct((n_rows, D_POOL), F32),
        ],
        compiler_params=pltpu.CompilerParams(
            dimension_semantics=("parallel",), vmem_limit_bytes=VMEM_LIMIT),
        name="inproj",
    )(x2d, w, hg, e, *rope)


def _attn_kernel(q_ref, k_ref, vt_ref, kmeta_ref, vtmeta_ref,
                 wo_ref, gmix_ref, wi_ref, gffn_ref, wd_ref,
                 o_ref, wo_out, wi_out, wd_out, kx_ref, vx_ref, s_ref,
                 *, tq, tk, seq):
    wo_out[...] = (wo_ref[...] * gmix_ref[...]).astype(BF16)
    wi_out[...] = (wi_ref[...] * gffn_ref[...]).astype(BF16)
    wd_out[...] = wd_ref[...].astype(BF16)

    n_qt = seq // tq
    nq = 2 * tq
    n_chunks = seq // tk
    n_keys = seq + N_META
    bounds = [(c * tk, (c + 1) * tk) for c in range(n_chunks)]
    bounds[-1] = (bounds[-1][0], n_keys)

    last_lo = bounds[-1][0]
    kx_ref[0:seq - last_lo, :] = k_ref[last_lo:seq, :]
    kx_ref[seq - last_lo:n_keys - last_lo, :] = kmeta_ref[...]
    vx_ref[:, 0:seq - last_lo] = vt_ref[0, 0, :, last_lo:seq]
    vx_ref[:, seq - last_lo:seq - last_lo + LANES] = vtmeta_ref[0]

    def keys(c):
        lo, hi = bounds[c]
        return kx_ref[0:hi - lo, :] if c == n_chunks - 1 else k_ref[lo:hi, :]

    def values_t(c):
        lo, hi = bounds[c]
        return vx_ref[...] if c == n_chunks - 1 else vt_ref[0, 0, :, lo:hi]

    def weights(j, pair):
        rows = pl.ds(pl.multiple_of(j * tq, tq), tq)
        qt = q_ref[rows, pair * LANES:(pair + 1) * LANES].astype(F32).T.astype(BF16)
        z = jnp.zeros((HEAD_DIM, tq), BF16)
        return jnp.concatenate([jnp.concatenate([qt[0:HEAD_DIM], z], axis=0),
                                jnp.concatenate([qt[HEAD_DIM:], z], axis=0)], axis=1)

    def score_chunk(w, slot, c, mx):
        lo, hi = bounds[c]
        s = jnp.dot(keys(c), w, preferred_element_type=F32)
        s_ref[slot, lo:hi, :] = s
        return jnp.maximum(mx, jnp.max(s, axis=0, keepdims=True))

    def pv_chunk(slot, c, m, acc):
        lo, hi = bounds[c]
        p = jnp.exp2(s_ref[slot, lo:hi, :] - m).astype(BF16)
        vt_c = values_t(c)
        if p.shape[0] < vt_c.shape[1]:
            p = jnp.concatenate([p, jnp.zeros((vt_c.shape[1] - p.shape[0], nq), BF16)], axis=0)
        return acc + jnp.dot(vt_c, p, preferred_element_type=F32)

    def stage(w_next, slot_next, slot_cur, m_cur):
        mx = jnp.full((1, nq), -jnp.inf, F32)
        acc = jnp.zeros((VT_ROWS, nq), F32)
        for c in range(n_chunks):
            mx = score_chunk(w_next, slot_next, c, mx)
            acc = pv_chunk(slot_cur, c, m_cur, acc)
        return mx, acc

    def finish(acc, j, pair):
        o = acc[0:HEAD_DIM] * (1.0 / acc[HEAD_DIM:HEAD_DIM + 1])
        both = jnp.concatenate([o[:, :tq], o[:, tq:]], axis=0)
        o_ref[pl.ds(pl.multiple_of(j * tq, tq), tq), pair * LANES:(pair + 1) * LANES] = both.T

    w0 = weights(0, 0)
    m0 = jnp.full((1, nq), -jnp.inf, F32)
    for c in range(n_chunks):
        m0 = score_chunk(w0, 0, c, m0)

    def body(jj, m_cur):
        for t in range(TILES_PER_ITER):
            j = jj * TILES_PER_ITER + t
            for pair in range(2):
                nxt = (j, 1) if pair == 0 else (jnp.minimum(j + 1, n_qt - 1), 0)
                m_cur, acc = stage(weights(*nxt), 1 - pair, pair, m_cur)
                finish(acc, j, pair)
        return m_cur

    lax.fori_loop(0, n_qt // TILES_PER_ITER, body, m0)


def _attention(qt, k2, vt, kmeta2, vtmeta, w_out, g_mix, w_ffn_in, g_ffn, w_ffn_down,
               n_batch, tq, tk):
    n_rows = k2.shape[0]
    seq = n_rows // n_batch
    n_qt = seq // tq
    gw = KV_GROUP * HEAD_DIM
    n_steps = n_batch * N_KV_HEADS
    kernel = functools.partial(_attn_kernel, tq=tq, tk=tk, seq=seq)

    def slab(w):
        rows = w.shape[0] // n_steps
        assert rows * n_steps == w.shape[0] and rows % 16 == 0, w.shape
        return pl.BlockSpec((rows, w.shape[1]), lambda b, g: (b * N_KV_HEADS + g, 0))

    weights_in = (w_out, g_mix, w_ffn_in, g_ffn, w_ffn_down)
    weights_out = (w_out, w_ffn_in, w_ffn_down)
    return pl.pallas_call(
        kernel,
        grid=(n_batch, N_KV_HEADS),
        in_specs=[
            pl.BlockSpec((seq, gw), lambda b, g: (b, g)),
            pl.BlockSpec((seq, LANES), lambda b, g: (b, g)),
            pl.BlockSpec((1, 1, VT_ROWS, seq), lambda b, g: (b, g, 0, 0)),
            pl.BlockSpec((N_META, LANES), lambda b, g: (0, g)),
            pl.BlockSpec((1, VT_ROWS, LANES), lambda b, g: (g, 0, 0)),
        ] + [slab(w) for w in weights_in],
        out_specs=[pl.BlockSpec((seq, gw), lambda b, g: (b, g))] + [slab(w) for w in weights_out],
        out_shape=[jax.ShapeDtypeStruct((n_rows, D_ATTN), F32)]
        + [jax.ShapeDtypeStruct(w.shape, BF16) for w in weights_out],
        scratch_shapes=[pltpu.VMEM((tk + N_META, LANES), BF16),
                        pltpu.VMEM((VT_ROWS, tk + LANES), BF16),
                        pltpu.VMEM((2, seq + N_META, 2 * tq), F32)],
        compiler_params=pltpu.CompilerParams(
            dimension_semantics=("parallel", "parallel"), vmem_limit_bytes=VMEM_LIMIT),
        name="attention",
    )(qt, k2, vt, kmeta2, vtmeta, *weights_in)


def _mix_ffn_kernel(x_ref, a_ref, u_ref, uprev_ref, unext_ref, umeta_ref,
                    wp_ref, wo_ref, wi_ref, wd_ref, gl_ref,
                    o_ref, ext_ref, act_ref, *, tm, tpb, seq_total):
    j = pl.program_id(0) % tpb

    a = a_ref[...]
    proj_a = jnp.dot(a.astype(BF16), wo_ref[0:D_ATTN, :], preferred_element_type=F32) * _rms(a)

    ext_ref[POOL_HALO:POOL_HALO + tm, :] = u_ref[...]
    ext_ref[0:POOL_HALO, :] = jnp.where(j == 0, umeta_ref[N_META - POOL_HALO:N_META, :],
                                        uprev_ref[...])
    ext_ref[POOL_HALO + tm:2 * POOL_HALO + tm, :] = jnp.where(j == tpb - 1, 0.0, unext_ref[...])
    ext_ref[2 * POOL_HALO + tm:, :] = jnp.zeros((EXT_SLACK, D_POOL), F32)

    t8 = j * tm + (tm - POOL_HALO + N_META) + lax.broadcasted_iota(
        jnp.int32, (POOL_HALO, POOL_GROUP), 0)
    diffs = []
    for gi, w in enumerate(POOL_WINDOWS):
        half = w // 2
        xg = ext_ref[:, gi * POOL_GROUP:(gi + 1) * POOL_GROUP]
        s, k, n = xg, 1, tm + 2 * POOL_HALO + EXT_SLACK
        while k < half:
            n -= POOL_HALO
            s = s[0:n] + s[k:k + n]
            k *= 2
        win = s[POOL_HALO - half:POOL_HALO - half + tm] + s[POOL_HALO:POOL_HALO + tm]
        u_self = xg[POOL_HALO:POOL_HALO + tm]
        cnt8 = jnp.minimum(t8 + half, seq_total) - (t8 - half)
        body = win[:tm - POOL_HALO] * (1.0 / w) - u_self[:tm - POOL_HALO]
        tail = win[tm - POOL_HALO:] * (1.0 / cnt8.astype(F32)) - u_self[tm - POOL_HALO:]
        diffs.append(jnp.concatenate([body, tail], axis=0))
    pooled = []
    for pr in range(2):
        d = jnp.concatenate(diffs[2 * pr:2 * pr + 2], axis=1).astype(BF16)
        pooled.append(jnp.dot(d, wp_ref[pr], preferred_element_type=F32))
    y = jnp.concatenate(pooled, axis=1)
    proj_p = jnp.dot(y.astype(BF16), wo_ref[D_ATTN:, :], preferred_element_type=F32) * _rms(y)
    h1 = x_ref[...] + proj_a + proj_p

    h1b = h1.astype(BF16)
    r1 = _rms(h1)
    for c in range(D_FF // FF_CHUNK):
        cols = slice(c * FF_CHUNK, (c + 1) * FF_CHUNK)
        gate = jnp.dot(h1b, wi_ref[:, cols], preferred_element_type=F32) * r1
        up = jnp.dot(h1b, wi_ref[:, D_FF + c * FF_CHUNK:D_FF + (c + 1) * FF_CHUNK],
                     preferred_element_type=F32) * r1
        act_ref[:, cols] = (gate * jax.nn.sigmoid(gate) * up).astype(BF16)
    h3 = h1 + jnp.dot(act_ref[...], wd_ref[...], preferred_element_type=F32)
    o_ref[...] = h3 * _rms(h3) * gl_ref[...]


def _mix_ffn(x2d, attn, u, umeta, wp, wo, wi, wd, gl, n_batch, tm):
    n_rows = x2d.shape[0]
    tiles = n_rows // tm
    tpb = tiles // n_batch
    hb = tm // POOL_HALO
    n_hblocks = n_rows // POOL_HALO
    kernel = functools.partial(_mix_ffn_kernel, tm=tm, tpb=tpb,
                               seq_total=N_META + n_rows // n_batch)

    def const(shape):
        zeros = (0,) * len(shape)
        return pl.BlockSpec(shape, lambda i: zeros, pipeline_mode=pl.Buffered(1))

    return pl.pallas_call(
        kernel,
        grid=(tiles,),
        in_specs=[
            pl.BlockSpec((tm, D_MODEL), lambda i: (i, 0)),
            pl.BlockSpec((tm, D_ATTN), lambda i: (i, 0)),
            pl.BlockSpec((tm, D_POOL), lambda i: (i, 0)),
            pl.BlockSpec((POOL_HALO, D_POOL), lambda i: (jnp.maximum(i * hb - 1, 0), 0)),
            pl.BlockSpec((POOL_HALO, D_POOL),
                         lambda i: (jnp.minimum((i + 1) * hb, n_hblocks - 1), 0)),
            const((N_META, D_POOL)),
            const((2, MXU_DIM, MXU_DIM)),
            const((D_MODEL, D_MODEL)),
            const((D_MODEL, 2 * D_FF)),
            const((D_FF, D_MODEL)),
            const((1, D_MODEL)),
        ],
        out_specs=pl.BlockSpec((tm, D_MODEL), lambda i: (i, 0)),
        out_shape=jax.ShapeDtypeStruct((n_rows, D_MODEL), F32),
        scratch_shapes=[pltpu.VMEM((tm + 2 * POOL_HALO + EXT_SLACK, D_POOL), F32),
                        pltpu.VMEM((tm, D_FF), BF16)],
        compiler_params=pltpu.CompilerParams(
            dimension_semantics=("parallel",), vmem_limit_bytes=VMEM_LIMIT),
        name="mix_ffn",
    )(x2d, attn, u, u, u, umeta, wp, wo, wi, wd, gl)


def _rope_tables(n_real):
    rows = n_real // GRID_W
    freqs = ROPE_THETA ** (-np.arange(ROPE_AXIS_PAIRS, dtype=np.float64) / ROPE_AXIS_PAIRS)
    ar = (np.arange(rows) - rows // 2).astype(np.float64)[:, None] * freqs
    ac = (np.arange(GRID_W) - GRID_W // 2).astype(np.float64)[:, None] * freqs
    zr = np.zeros((rows, 2 * ROPE_AXIS_PAIRS))
    zc = np.zeros((GRID_W, 2 * ROPE_AXIS_PAIRS))
    tables = (np.concatenate([np.cos(ar), np.cos(ar), zr], axis=1),
              np.concatenate([-np.sin(ar), np.sin(ar), zr], axis=1),
              np.concatenate([zc, np.cos(ac), np.cos(ac)], axis=1),
              np.concatenate([zc, -np.sin(ac), np.sin(ac)], axis=1))
    return tuple(jnp.asarray(np.tile(t, (1, LANES // HEAD_DIM)), dtype=F32) for t in tables)


def kernel(x, meta_tokens, norm_mix, w_in, q_norm, k_norm, attn_out_norm, w_pool,
           pool_scale, w_out, norm_ffn, w_ffn_in, w_ffn_down, norm_final):
    n_batch, n_real, _ = x.shape
    assert n_real % TM_PROJ == 0 and n_real % TQ == 0 and n_real % TM_FFN == 0
    assert w_in.shape[0] == 1, "single layer"
    x2d = x.reshape(n_batch * n_real, D_MODEL)

    w_in_b = (norm_mix[0][:, None] * w_in[0]).astype(BF16)
    head_gain = jnp.concatenate([jnp.tile(q_norm[0], N_HEADS) * (HEAD_DIM ** -0.5 * LOG2_E),
                                 jnp.tile(k_norm[0], N_KV_HEADS)]).reshape(1, D_QK)
    blk = np.arange(MXU_DIM) // HEAD_DIM
    e_heads = jnp.asarray(blk[:, None] == blk[None, :], dtype=BF16)
    rope = _rope_tables(n_real)

    wp = jnp.zeros((2, MXU_DIM, MXU_DIM), F32)
    for gi in range(len(POOL_WINDOWS)):
        r0 = (gi % 2) * POOL_GROUP
        wp = wp.at[gi // 2, r0:r0 + POOL_GROUP, r0:r0 + POOL_GROUP].set(w_pool[0, gi])
    wp = wp.astype(BF16)
    g_mix = jnp.concatenate([attn_out_norm[0], pool_scale[0]]).reshape(D_MODEL, 1)
    g_ffn = norm_ffn[0].reshape(D_MODEL, 1)

    qt, k2, vt, u = _inproj(x2d, w_in_b, head_gain, e_heads, rope, n_batch, TM_PROJ)
    meta_pad = jnp.zeros((TQ, D_MODEL), F32).at[:N_META].set(meta_tokens)
    ones_row = jnp.asarray(np.ones((TQ // GRID_W, LANES), np.float32))
    zero_row = jnp.asarray(np.zeros((TQ // GRID_W, LANES), np.float32))
    zero_col = jnp.asarray(np.zeros((GRID_W, LANES), np.float32))
    _, kmeta2, vtmeta, umeta = _inproj(meta_pad, w_in_b, head_gain, e_heads,
                                       (ones_row, zero_row, zero_col, zero_col), 1, TQ)
    kmeta2 = kmeta2[:N_META]
    lane = jnp.arange(LANES)
    vtmeta = jnp.where(lane[None, None, :] < N_META, vtmeta[0, :, :, :LANES], 0).astype(BF16)
    umeta = umeta[:N_META]

    attn, w_out_b, wi, wd = _attention(qt, k2, vt, kmeta2, vtmeta,
                                       w_out[0], g_mix, w_ffn_in[0], g_ffn, w_ffn_down[0],
                                       n_batch, TQ, TK)

    out = _mix_ffn(x2d, attn, u, umeta, wp, w_out_b, wi, wd,
                   norm_final.reshape(1, D_MODEL), n_batch, TM_FFN)
    return out.reshape(n_batch, n_real, D_MODEL)
```

```python
import functools

import jax
import jax.numpy as jnp
import numpy as np
from jax import lax
from jax.experimental import pallas as pl
from jax.experimental.pallas import tpu as pltpu

D_MODEL = 1024
N_META = 16
GRID_W = 64
N_HEADS = 8
N_KV_HEADS = 2
HEAD_DIM = 64
KV_GROUP = N_HEADS // N_KV_HEADS
D_ATTN = N_HEADS * HEAD_DIM
D_KV = N_KV_HEADS * HEAD_DIM
D_QK = D_ATTN + D_KV
D_POOL = 512
POOL_WINDOWS = (2, 4, 8, 16)
POOL_GROUP = 128
POOL_HALO = 8
EXT_SLACK = 16
D_IN = D_ATTN + 2 * D_KV + D_POOL
D_FF = 2816
ROPE_AXIS_PAIRS = HEAD_DIM // 4
ROPE_THETA = 10000.0
EPS = 1e-6

LANES = 128
MXU_DIM = 256
VT_ROWS = HEAD_DIM + 16
VMEM_LIMIT = 56 * 1024 * 1024

LOG2_E = 1.4426950408889634

TM_PROJ = 1024
TQ = 256
TK = 256
TILES_PER_ITER = 8
TM_FFN = 512
FF_CHUNK = 256

F32 = jnp.float32
BF16 = jnp.bfloat16


def _rms(x):
    return lax.rsqrt(jnp.mean(x * x, axis=-1, keepdims=True) + EPS)


def _inproj_kernel(x_ref, w_ref, hg_ref, e_ref, cos_row_ref, sin_row_ref, cos_col_ref, sin_col_ref,
                   q_ref, k_ref, vt_ref, u_ref):
    x = x_ref[...]
    h = x.astype(BF16)
    r = _rms(x)
    e = e_ref[...]

    def table(row_ref, col_ref):
        col = col_ref[...]
        return jnp.concatenate([row_ref[g:g + 1, :] + col for g in range(row_ref.shape[0])],
                               axis=0)

    cos = table(cos_row_ref, cos_col_ref)
    sin = table(sin_row_ref, sin_col_ref)
    lane = lax.broadcasted_iota(jnp.int32, cos.shape, 1)
    first_half = (lane & ROPE_AXIS_PAIRS) == 0

    def project(c0, width):
        return jnp.dot(h, w_ref[:, c0:c0 + width], preferred_element_type=F32) * r

    def head_norm_rope(p, c0):
        width = p.shape[1]
        ss = jnp.dot((p * p).astype(BF16), e[:width, :width], preferred_element_type=F32)
        pn = p * lax.rsqrt(ss * (1.0 / HEAD_DIM) + EPS) * hg_ref[:, c0:c0 + width]
        out = []
        for c in range(width // LANES):
            xc = pn[:, c * LANES:(c + 1) * LANES]
            partner = jnp.where(first_half,
                                pltpu.roll(xc, LANES - ROPE_AXIS_PAIRS, 1),
                                pltpu.roll(xc, ROPE_AXIS_PAIRS, 1))
            out.append(xc * cos + partner * sin)
        return out

    def q_epilogue(p, blk):
        roped = head_norm_rope(p, blk * MXU_DIM)
        for c, rc in enumerate(roped):
            c0 = blk * MXU_DIM + c * LANES
            q_ref[:, c0:c0 + LANES] = rc.astype(BF16)

    def kv_epilogue(kv):
        kc, = head_norm_rope(kv[:, :D_KV], D_ATTN)
        k_ref[:, 0:LANES] = kc.astype(BF16)
        k_ref[:, LANES:2 * LANES] = pltpu.roll(kc, HEAD_DIM, 1).astype(BF16)
        vt = kv[:, D_KV:].T
        ones = jnp.ones((VT_ROWS - HEAD_DIM, vt.shape[1]), BF16)
        for g in range(N_KV_HEADS):
            vt_ref[0, g, 0:HEAD_DIM, :] = vt[g * HEAD_DIM:(g + 1) * HEAD_DIM].astype(BF16)
            vt_ref[0, g, HEAD_DIM:VT_ROWS, :] = ones

    p_q0 = project(0, MXU_DIM)
    p_q1 = project(MXU_DIM, MXU_DIM)
    q_epilogue(p_q0, 0)
    p_kv = project(D_ATTN, 2 * D_KV)
    q_epilogue(p_q1, 1)
    u_ref[:, 0:MXU_DIM] = project(D_QK + D_KV, MXU_DIM)
    kv_epilogue(p_kv)
    u_ref[:, MXU_DIM:] = project(D_QK + D_KV + MXU_DIM, MXU_DIM)


def _inproj(x2d, w, hg, e, rope, n_batch, tm):
    n_rows = x2d.shape[0]
    tiles = n_rows // tm
    tpb = tiles // n_batch
    assert tm % GRID_W == 0
    rpt = tm // GRID_W
    n_pos = rope[0].shape[0] // rpt
    const = lambda i: (0, 0)
    row_spec = pl.BlockSpec((rpt, LANES), lambda i: (i % n_pos, 0))
    col_spec = pl.BlockSpec((GRID_W, LANES), const)
    return pl.pallas_call(
        _inproj_kernel,
        grid=(tiles,),
        in_specs=[
            pl.BlockSpec((tm, D_MODEL), lambda i: (i, 0)),
            pl.BlockSpec((D_MODEL, D_IN), const),
            pl.BlockSpec((1, D_QK), const),
            pl.BlockSpec((MXU_DIM, MXU_DIM), const),
            row_spec, row_spec, col_spec, col_spec,
        ],
        out_specs=[
            pl.BlockSpec((tm, D_ATTN), lambda i: (i, 0)),
            pl.BlockSpec((tm, 2 * D_KV), lambda i: (i, 0)),
            pl.BlockSpec((1, N_KV_HEADS, VT_ROWS, tm), lambda i: (i // tpb, 0, 0, i % tpb)),
            pl.BlockSpec((tm, D_POOL), lambda i: (i, 0)),
        ],
        out_shape=[
            jax.ShapeDtypeStruct((n_rows, D_ATTN), BF16),
            jax.ShapeDtypeStruct((n_rows, 2 * D_KV), BF16),
            jax.ShapeDtypeStruct((n_batch, N_KV_HEADS, VT_ROWS, tpb * tm), BF16),
            jax.ShapeDtypeStruct((n_rows, D_POOL), F32),
        ],
        compiler_params=pltpu.CompilerParams(
            dimension_semantics=("parallel",), vmem_limit_bytes=VMEM_LIMIT),
        name="inproj",
    )(x2d, w, hg, e, *rope)


def _attn_kernel(q_ref, k_ref, vt_ref, kmeta_ref, vtmeta_ref,
                 wo_ref, gmix_ref, wi_ref, gffn_ref, wd_ref,
                 o_ref, wo_out, wi_out, wd_out, kx_ref, vx_ref, s_ref,
                 *, tq, tk, seq):
    wo_out[...] = (wo_ref[...] * gmix_ref[...]).astype(BF16)
    wi_out[...] = (wi_ref[...] * gffn_ref[...]).astype(BF16)
    wd_out[...] = wd_ref[...].astype(BF16)

    n_qt = seq // tq
    nq = 2 * tq
    n_chunks = seq // tk
    n_keys = seq + N_META
    bounds = [(c * tk, (c + 1) * tk) for c in range(n_chunks)]
    bounds[-1] = (bounds[-1][0], n_keys)

    last_lo = bounds[-1][0]
    kx_ref[0:seq - last_lo, :] = k_ref[last_lo:seq, :]
    kx_ref[seq - last_lo:n_keys - last_lo, :] = kmeta_ref[...]
    vx_ref[:, 0:seq - last_lo] = vt_ref[0, 0, :, last_lo:seq]
    vx_ref[:, seq - last_lo:seq - last_lo + LANES] = vtmeta_ref[0]

    def keys(c):
        lo, hi = bounds[c]
        return kx_ref[0:hi - lo, :] if c == n_chunks - 1 else k_ref[lo:hi, :]

    def values_t(c):
        lo, hi = bounds[c]
        return vx_ref[...] if c == n_chunks - 1 else vt_ref[0, 0, :, lo:hi]

    def weights(j, pair):
        rows = pl.ds(pl.multiple_of(j * tq, tq), tq)
        qt = q_ref[rows, pair * LANES:(pair + 1) * LANES].astype(F32).T.astype(BF16)
        z = jnp.zeros((HEAD_DIM, tq), BF16)
        return jnp.concatenate([jnp.concatenate([qt[0:HEAD_DIM], z], axis=0),
                                jnp.concatenate([qt[HEAD_DIM:], z], axis=0)], axis=1)

    def score_chunk(w, slot, c, mx):
        lo, hi = bounds[c]
        s = jnp.dot(keys(c), w, preferred_element_type=F32)
        s_ref[slot, lo:hi, :] = s
        return jnp.maximum(mx, jnp.max(s, axis=0, keepdims=True))

    def pv_chunk(slot, c, m, acc):
        lo, hi = bounds[c]
        p = jnp.exp2(s_ref[slot, lo:hi, :] - m).astype(BF16)
        vt_c = values_t(c)
        if p.shape[0] < vt_c.shape[1]:
            p = jnp.concatenate([p, jnp.zeros((vt_c.shape[1] - p.shape[0], nq), BF16)], axis=0)
        return acc + jnp.dot(vt_c, p, preferred_element_type=F32)

    def stage(w_next, slot_next, slot_cur, m_cur):
        mx = jnp.full((1, nq), -jnp.inf, F32)
        acc = jnp.zeros((VT_ROWS, nq), F32)
        for c in range(n_chunks):
            mx = score_chunk(w_next, slot_next, c, mx)
            acc = pv_chunk(slot_cur, c, m_cur, acc)
        return mx, acc

    def finish(acc, j, pair):
        o = acc[0:HEAD_DIM] * (1.0 / acc[HEAD_DIM:HEAD_DIM + 1])
        both = jnp.concatenate([o[:, :tq], o[:, tq:]], axis=0)
        o_ref[pl.ds(pl.multiple_of(j * tq, tq), tq), pair * LANES:(pair + 1) * LANES] = both.T

    w0 = weights(0, 0)
    m0 = jnp.full((1, nq), -jnp.inf, F32)
    for c in range(n_chunks):
        m0 = score_chunk(w0, 0, c, m0)

    def body(jj, m_cur):
        for t in range(TILES_PER_ITER):
            j = jj * TILES_PER_ITER + t
            for pair in range(2):
                nxt = (j, 1) if pair == 0 else (jnp.minimum(j + 1, n_qt - 1), 0)
                m_cur, acc = stage(weights(*nxt), 1 - pair, pair, m_cur)
                finish(acc, j, pair)
        return m_cur

    lax.fori_loop(0, n_qt // TILES_PER_ITER, body, m0)


def _attention(qt, k2, vt, kmeta2, vtmeta, w_out, g_mix, w_ffn_in, g_ffn, w_ffn_down,
               n_batch, tq, tk):
    n_rows = k2.shape[0]
    seq = n_rows // n_batch
    n_qt = seq // tq
    gw = KV_GROUP * HEAD_DIM
    n_steps = n_batch * N_KV_HEADS
    kernel = functools.partial(_attn_kernel, tq=tq, tk=tk, seq=seq)

    def slab(w):
        rows = w.shape[0] // n_steps
        assert rows * n_steps == w.shape[0] and rows % 16 == 0, w.shape
        return pl.BlockSpec((rows, w.shape[1]), lambda b, g: (b * N_KV_HEADS + g, 0))

    weights_in = (w_out, g_mix, w_ffn_in, g_ffn, w_ffn_down)
    weights_out = (w_out, w_ffn_in, w_ffn_down)
    return pl.pallas_call(
        kernel,
        grid=(n_batch, N_KV_HEADS),
        in_specs=[
            pl.BlockSpec((seq, gw), lambda b, g: (b, g)),
            pl.BlockSpec((seq, LANES), lambda b, g: (b, g)),
            pl.BlockSpec((1, 1, VT_ROWS, seq), lambda b, g: (b, g, 0, 0)),
            pl.BlockSpec((N_META, LANES), lambda b, g: (0, g)),
            pl.BlockSpec((1, VT_ROWS, LANES), lambda b, g: (g, 0, 0)),
        ] + [slab(w) for w in weights_in],
        out_specs=[pl.BlockSpec((seq, gw), lambda b, g: (b, g))] + [slab(w) for w in weights_out],
        out_shape=[jax.ShapeDtypeStruct((n_rows, D_ATTN), F32)]
        + [jax.ShapeDtypeStruct(w.shape, BF16) for w in weights_out],
        scratch_shapes=[pltpu.VMEM((tk + N_META, LANES), BF16),
                        pltpu.VMEM((VT_ROWS, tk + LANES), BF16),
                        pltpu.VMEM((2, seq + N_META, 2 * tq), F32)],
        compiler_params=pltpu.CompilerParams(
            dimension_semantics=("parallel", "parallel"), vmem_limit_bytes=VMEM_LIMIT),
        name="attention",
    )(qt, k2, vt, kmeta2, vtmeta, *weights_in)


def _mix_ffn_kernel(x_ref, a_ref, u_ref, uprev_ref, unext_ref, umeta_ref,
                    wp_ref, wo_ref, wi_ref, wd_ref, gl_ref,
                    o_ref, ext_ref, act_ref, *, tm, tpb, seq_total):
    j = pl.program_id(0) % tpb

    a = a_ref[...]
    proj_a = jnp.dot(a.astype(BF16), wo_ref[0:D_ATTN, :], preferred_element_type=F32) * _rms(a)

    ext_ref[POOL_HALO:POOL_HALO + tm, :] = u_ref[...]
    ext_ref[0:POOL_HALO, :] = jnp.where(j == 0, umeta_ref[N_META - POOL_HALO:N_META, :],
                                        uprev_ref[...])
    ext_ref[POOL_HALO + tm:2 * POOL_HALO + tm, :] = jnp.where(j == tpb - 1, 0.0, unext_ref[...])
    ext_ref[2 * POOL_HALO + tm:, :] = jnp.zeros((EXT_SLACK, D_POOL), F32)

    t8 = j * tm + (tm - POOL_HALO + N_META) + lax.broadcasted_iota(
        jnp.int32, (POOL_HALO, POOL_GROUP), 0)
    diffs = []
    for gi, w in enumerate(POOL_WINDOWS):
        half = w // 2
        xg = ext_ref[:, gi * POOL_GROUP:(gi + 1) * POOL_GROUP]
        s, k, n = xg, 1, tm + 2 * POOL_HALO + EXT_SLACK
        while k < half:
            n -= POOL_HALO
            s = s[0:n] + s[k:k + n]
            k *= 2
        win = s[POOL_HALO - half:POOL_HALO - half + tm] + s[POOL_HALO:POOL_HALO + tm]
        u_self = xg[POOL_HALO:POOL_HALO + tm]
        cnt8 = jnp.minimum(t8 + half, seq_total) - (t8 - half)
        body = win[:tm - POOL_HALO] * (1.0 / w) - u_self[:tm - POOL_HALO]
        tail = win[tm - POOL_HALO:] * (1.0 / cnt8.astype(F32)) - u_self[tm - POOL_HALO:]
        diffs.append(jnp.concatenate([body, tail], axis=0))
    pooled = []
    for pr in range(2):
        d = jnp.concatenate(diffs[2 * pr:2 * pr + 2], axis=1).astype(BF16)
        pooled.append(jnp.dot(d, wp_ref[pr], preferred_element_type=F32))
    y = jnp.concatenate(pooled, axis=1)
    proj_p = jnp.dot(y.astype(BF16), wo_ref[D_ATTN:, :], preferred_element_type=F32) * _rms(y)
    h1 = x_ref[...] + proj_a + proj_p

    h1b = h1.astype(BF16)
    r1 = _rms(h1)
    for c in range(D_FF // FF_CHUNK):
        cols = slice(c * FF_CHUNK, (c + 1) * FF_CHUNK)
        gate = jnp.dot(h1b, wi_ref[:, cols], preferred_element_type=F32) * r1
        up = jnp.dot(h1b, wi_ref[:, D_FF + c * FF_CHUNK:D_FF + (c + 1) * FF_CHUNK],
                     preferred_element_type=F32) * r1
        act_ref[:, cols] = (gate * jax.nn.sigmoid(gate) * up).astype(BF16)
    h3 = h1 + jnp.dot(act_ref[...], wd_ref[...], preferred_element_type=F32)
    o_ref[...] = h3 * _rms(h3) * gl_ref[...]


def _mix_ffn(x2d, attn, u, umeta, wp, wo, wi, wd, gl, n_batch, tm):
    n_rows = x2d.shape[0]
    tiles = n_rows // tm
    tpb = tiles // n_batch
    hb = tm // POOL_HALO
    n_hblocks = n_rows // POOL_HALO
    kernel = functools.partial(_mix_ffn_kernel, tm=tm, tpb=tpb,
                               seq_total=N_META + n_rows // n_batch)

    def const(shape):
        zeros = (0,) * len(shape)
        return pl.BlockSpec(shape, lambda i: zeros, pipeline_mode=pl.Buffered(1))

    return pl.pallas_call(
        kernel,
        grid=(tiles,),
        in_specs=[
            pl.BlockSpec((tm, D_MODEL), lambda i: (i, 0)),
            pl.BlockSpec((tm, D_ATTN), lambda i: (i, 0)),
            pl.BlockSpec((tm, D_POOL), lambda i: (i, 0)),
            pl.BlockSpec((POOL_HALO, D_POOL), lambda i: (jnp.maximum(i * hb - 1, 0), 0)),
            pl.BlockSpec((POOL_HALO, D_POOL),
                         lambda i: (jnp.minimum((i + 1) * hb, n_hblocks - 1), 0)),
            const((N_META, D_POOL)),
            const((2, MXU_DIM, MXU_DIM)),
            const((D_MODEL, D_MODEL)),
            const((D_MODEL, 2 * D_FF)),
            const((D_FF, D_MODEL)),
            const((1, D_MODEL)),
        ],
        out_specs=pl.BlockSpec((tm, D_MODEL), lambda i: (i, 0)),
        out_shape=jax.ShapeDtypeStruct((n_rows, D_MODEL), F32),
        scratch_shapes=[pltpu.VMEM((tm + 2 * POOL_HALO + EXT_SLACK, D_POOL), F32),
                        pltpu.VMEM((tm, D_FF), BF16)],
        compiler_params=pltpu.CompilerParams(
            dimension_semantics=("parallel",), vmem_limit_bytes=VMEM_LIMIT),
        name="mix_ffn",
    )(x2d, attn, u, u, u, umeta, wp, wo, wi, wd, gl)


def _rope_tables(n_real):
    rows = n_real // GRID_W
    freqs = ROPE_THETA ** (-np.arange(ROPE_AXIS_PAIRS, dtype=np.float64) / ROPE_AXIS_PAIRS)
    ar = (np.arange(rows) - rows // 2).astype(np.float64)[:, None] * freqs
    ac = (np.arange(GRID_W) - GRID_W // 2).astype(np.float64)[:, None] * freqs
    zr = np.zeros((rows, 2 * ROPE_AXIS_PAIRS))
    zc = np.zeros((GRID_W, 2 * ROPE_AXIS_PAIRS))
    tables = (np.concatenate([np.cos(ar), np.cos(ar), zr], axis=1),
              np.concatenate([-np.sin(ar), np.sin(ar), zr], axis=1),
              np.concatenate([zc, np.cos(ac), np.cos(ac)], axis=1),
              np.concatenate([zc, -np.sin(ac), np.sin(ac)], axis=1))
    return tuple(jnp.asarray(np.tile(t, (1, LANES // HEAD_DIM)), dtype=F32) for t in tables)


def kernel(x, meta_tokens, norm_mix, w_in, q_norm, k_norm, attn_out_norm, w_pool,
           pool_scale, w_out, norm_ffn, w_ffn_in, w_ffn_down, norm_final):
    n_batch, n_real, _ = x.shape
    assert n_real % TM_PROJ == 0 and n_real % TQ == 0 and n_real % TM_FFN == 0
    assert w_in.shape[0] == 1, "single layer"
    x2d = x.reshape(n_batch * n_real, D_MODEL)

    w_in_b = (norm_mix[0][:, None] * w_in[0]).astype(BF16)
    head_gain = jnp.concatenate([jnp.tile(q_norm[0], N_HEADS) * (HEAD_DIM ** -0.5 * LOG2_E),
                                 jnp.tile(k_norm[0], N_KV_HEADS)]).reshape(1, D_QK)
    blk = np.arange(MXU_DIM) // HEAD_DIM
    e_heads = jnp.asarray(blk[:, None] == blk[None, :], dtype=BF16)
    rope = _rope_tables(n_real)

    wp = jnp.zeros((2, MXU_DIM, MXU_DIM), F32)
    for gi in range(len(POOL_WINDOWS)):
        r0 = (gi % 2) * POOL_GROUP
        wp = wp.at[gi // 2, r0:r0 + POOL_GROUP, r0:r0 + POOL_GROUP].set(w_pool[0, gi])
    wp = wp.astype(BF16)
    g_mix = jnp.concatenate([attn_out_norm[0], pool_scale[0]]).reshape(D_MODEL, 1)
    g_ffn = norm_ffn[0].reshape(D_MODEL, 1)

    qt, k2, vt, u = _inproj(x2d, w_in_b, head_gain, e_heads, rope, n_batch, TM_PROJ)
    meta_pad = jnp.zeros((TQ, D_MODEL), F32).at[:N_META].set(meta_tokens)
    ones_row = jnp.asarray(np.ones((TQ // GRID_W, LANES), np.float32))
    zero_row = jnp.asarray(np.zeros((TQ // GRID_W, LANES), np.float32))
    zero_col = jnp.asarray(np.zeros((GRID_W, LANES), np.float32))
    _, kmeta2, vtmeta, umeta = _inproj(meta_pad, w_in_b, head_gain, e_heads,
                                       (ones_row, zero_row, zero_col, zero_col), 1, TQ)
    kmeta2 = kmeta2[:N_META]
    lane = jnp.arange(LANES)
    vtmeta = jnp.where(lane[None, None, :] < N_META, vtmeta[0, :, :, :LANES], 0).astype(BF16)
    umeta = umeta[:N_META]

    attn, w_out_b, wi, wd = _attention(qt, k2, vt, kmeta2, vtmeta,
                                       w_out[0], g_mix, w_ffn_in[0], g_ffn, w_ffn_down[0],
                                       n_batch, TQ, TK)

    out = _mix_ffn(x2d, attn, u, umeta, wp, w_out_b, wi, wd,
                   norm_final.reshape(1, D_MODEL), n_batch, TM_FFN)
    return out.reshape(n_batch, n_real, D_MODEL)
```

```python
import functools

import jax
import jax.numpy as jnp
import numpy as np
from jax import lax
from jax.experimental import pallas as pl
from jax.experimental.pallas import tpu as pltpu

D_MODEL = 1024
N_META = 16
GRID_W = 64
N_HEADS = 8
N_KV_HEADS = 2
HEAD_DIM = 64
KV_GROUP = N_HEADS // N_KV_HEADS
D_ATTN = N_HEADS * HEAD_DIM
D_KV = N_KV_HEADS * HEAD_DIM
D_QK = D_ATTN + D_KV
D_POOL = 512
POOL_WINDOWS = (2, 4, 8, 16)
POOL_GROUP = 128
POOL_HALO = 8
EXT_SLACK = 16
D_IN = D_ATTN + 2 * D_KV + D_POOL
D_FF = 2816
ROPE_AXIS_PAIRS = HEAD_DIM // 4
ROPE_THETA = 10000.0
EPS = 1e-6

LANES = 128
MXU_DIM = 256
VT_ROWS = HEAD_DIM + 16
VMEM_LIMIT = 56 * 1024 * 1024

LOG2_E = 1.4426950408889634

TM_PROJ = 1024
TQ = 256
TK = 256
TILES_PER_ITER = 2
TM_FFN = 512
FF_CHUNK = 256

F32 = jnp.float32
BF16 = jnp.bfloat16


def _rms(x):
    return lax.rsqrt(jnp.mean(x * x, axis=-1, keepdims=True) + EPS)


def _inproj_kernel(x_ref, w_ref, hg_ref, e_ref, cos_row_ref, sin_row_ref, cos_col_ref, sin_col_ref,
                   q_ref, k_ref, vt_ref, u_ref):
    x = x_ref[...]
    h = x.astype(BF16)
    r = _rms(x)
    e = e_ref[...]

    def table(row_ref, col_ref):
        col = col_ref[...]
        return jnp.concatenate([row_ref[g:g + 1, :] + col for g in range(row_ref.shape[0])],
                               axis=0)

    cos = table(cos_row_ref, cos_col_ref)
    sin = table(sin_row_ref, sin_col_ref)
    lane = lax.broadcasted_iota(jnp.int32, cos.shape, 1)
    first_half = (lane & ROPE_AXIS_PAIRS) == 0

    def project(c0, width):
        return jnp.dot(h, w_ref[:, c0:c0 + width], preferred_element_type=F32) * r

    def head_norm_rope(p, c0):
        width = p.shape[1]
        ss = jnp.dot((p * p).astype(BF16), e[:width, :width], preferred_element_type=F32)
        pn = p * lax.rsqrt(ss * (1.0 / HEAD_DIM) + EPS) * hg_ref[:, c0:c0 + width]
        out = []
        for c in range(width // LANES):
            xc = pn[:, c * LANES:(c + 1) * LANES]
            partner = jnp.where(first_half,
                                pltpu.roll(xc, LANES - ROPE_AXIS_PAIRS, 1),
                                pltpu.roll(xc, ROPE_AXIS_PAIRS, 1))
            out.append(xc * cos + partner * sin)
        return out

    def q_epilogue(p, blk):
        roped = head_norm_rope(p, blk * MXU_DIM)
        for c, rc in enumerate(roped):
            c0 = blk * MXU_DIM + c * LANES
            q_ref[:, c0:c0 + LANES] = rc.astype(BF16)

    def kv_epilogue(kv):
        kc, = head_norm_rope(kv[:, :D_KV], D_ATTN)
        k_ref[:, 0:LANES] = kc.astype(BF16)
        k_ref[:, LANES:2 * LANES] = pltpu.roll(kc, HEAD_DIM, 1).astype(BF16)
        vt = kv[:, D_KV:].T
        ones = jnp.ones((VT_ROWS - HEAD_DIM, vt.shape[1]), BF16)
        for g in range(N_KV_HEADS):
            vt_ref[0, g, 0:HEAD_DIM, :] = vt[g * HEAD_DIM:(g + 1) * HEAD_DIM].astype(BF16)
            vt_ref[0, g, HEAD_DIM:VT_ROWS, :] = ones

    p_q0 = project(0, MXU_DIM)
    p_q1 = project(MXU_DIM, MXU_DIM)
    q_epilogue(p_q0, 0)
    p_kv = project(D_ATTN, 2 * D_KV)
    q_epilogue(p_q1, 1)
    u_ref[:, 0:MXU_DIM] = project(D_QK + D_KV, MXU_DIM)
    kv_epilogue(p_kv)
    u_ref[:, MXU_DIM:] = project(D_QK + D_KV + MXU_DIM, MXU_DIM)


def _inproj(x2d, w, hg, e, rope, n_batch, tm):
    n_rows = x2d.shape[0]
    tiles = n_rows // tm
    tpb = tiles // n_batch
    assert tm % GRID_W == 0
    rpt = tm // GRID_W
    n_pos = rope[0].shape[0] // rpt
    const = lambda i: (0, 0)
    row_spec = pl.BlockSpec((rpt, LANES), lambda i: (i % n_pos, 0))
    col_spec = pl.BlockSpec((GRID_W, LANES), const)
    return pl.pallas_call(
        _inproj_kernel,
        grid=(tiles,),
        in_specs=[
            pl.BlockSpec((tm, D_MODEL), lambda i: (i, 0)),
            pl.BlockSpec((D_MODEL, D_IN), const),
            pl.BlockSpec((1, D_QK), const),
            pl.BlockSpec((MXU_DIM, MXU_DIM), const),
            row_spec, row_spec, col_spec, col_spec,
        ],
        out_specs=[
            pl.BlockSpec((tm, D_ATTN), lambda i: (i, 0)),
            pl.BlockSpec((tm, 2 * D_KV), lambda i: (i, 0)),
            pl.BlockSpec((1, N_KV_HEADS, VT_ROWS, tm), lambda i: (i // tpb, 0, 0, i % tpb)),
            pl.BlockSpec((tm, D_POOL), lambda i: (i, 0)),
        ],
        out_shape=[
            jax.ShapeDtypeStruct((n_rows, D_ATTN), BF16),
            jax.ShapeDtypeStruct((n_rows, 2 * D_KV), BF16),
            jax.ShapeDtypeStruct((n_batch, N_KV_HEADS, VT_ROWS, tpb * tm), BF16),
            jax.ShapeDtypeStruct((n_rows, D_POOL), F32),
        ],
        compiler_params=pltpu.CompilerParams(
            dimension_semantics=("parallel",), vmem_limit_bytes=VMEM_LIMIT),
        name="inproj",
    )(x2d, w, hg, e, *rope)


def _attn_kernel(q_ref, k_ref, vt_ref, kmeta_ref, vtmeta_ref,
                 wo_ref, gmix_ref, wi_ref, gffn_ref, wd_ref,
                 o_ref, wo_out, wi_out, wd_out, kx_ref, vx_ref, s_ref,
                 *, tq, tk, seq):
    wo_out[...] = (wo_ref[...] * gmix_ref[...]).astype(BF16)
    wi_out[...] = (wi_ref[...] * gffn_ref[...]).astype(BF16)
    wd_out[...] = wd_ref[...].astype(BF16)

    n_qt = seq // tq
    nq = 2 * tq
    n_chunks = seq // tk
    n_keys = seq + N_META
    bounds = [(c * tk, (c + 1) * tk) for c in range(n_chunks)]
    bounds[-1] = (bounds[-1][0], n_keys)

    last_lo = bounds[-1][0]
    kx_ref[0:seq - last_lo, :] = k_ref[last_lo:seq, :]
    kx_ref[seq - last_lo:n_keys - last_lo, :] = kmeta_ref[...]
    vx_ref[:, 0:seq - last_lo] = vt_ref[0, 0, :, last_lo:seq]
    vx_ref[:, seq - last_lo:seq - last_lo + LANES] = vtmeta_ref[0]

    def keys(c):
        lo, hi = bounds[c]
        return kx_ref[0:hi - lo, :] if c == n_chunks - 1 else k_ref[lo:hi, :]

    def values_t(c):
        lo, hi = bounds[c]
        return vx_ref[...] if c == n_chunks - 1 else vt_ref[0, 0, :, lo:hi]

    def weights(j, pair):
        rows = pl.ds(pl.multiple_of(j * tq, tq), tq)
        qt = q_ref[rows, pair * LANES:(pair + 1) * LANES].astype(F32).T.astype(BF16)
        z = jnp.zeros((HEAD_DIM, tq), BF16)
        return jnp.concatenate([jnp.concatenate([qt[0:HEAD_DIM], z], axis=0),
                                jnp.concatenate([qt[HEAD_DIM:], z], axis=0)], axis=1)

    def score_chunk(w, slot, c, mx):
        lo, hi = bounds[c]
        s = jnp.dot(keys(c), w, preferred_element_type=F32)
        s_ref[slot, lo:hi, :] = s
        return jnp.maximum(mx, jnp.max(s, axis=0, keepdims=True))

    def pv_chunk(slot, c, m, acc):
        lo, hi = bounds[c]
        p = jnp.exp2(s_ref[slot, lo:hi, :] - m).astype(BF16)
        vt_c = values_t(c)
        if p.shape[0] < vt_c.shape[1]:
            p = jnp.concatenate([p, jnp.zeros((vt_c.shape[1] - p.shape[0], nq), BF16)], axis=0)
        return acc + jnp.dot(vt_c, p, preferred_element_type=F32)

    def stage(w_next, slot_next, slot_cur, m_cur):
        mx = jnp.full((1, nq), -jnp.inf, F32)
        acc = jnp.zeros((VT_ROWS, nq), F32)
        for c in range(n_chunks):
            mx = score_chunk(w_next, slot_next, c, mx)
            acc = pv_chunk(slot_cur, c, m_cur, acc)
        return mx, acc

    def finish(acc, j, pair):
        o = acc[0:HEAD_DIM] * (1.0 / acc[HEAD_DIM:HEAD_DIM + 1])
        both = jnp.concatenate([o[:, :tq], o[:, tq:]], axis=0)
        o_ref[pl.ds(pl.multiple_of(j * tq, tq), tq), pair * LANES:(pair + 1) * LANES] = both.T

    w0 = weights(0, 0)
    m0 = jnp.full((1, nq), -jnp.inf, F32)
    for c in range(n_chunks):
        m0 = score_chunk(w0, 0, c, m0)

    def body(jj, m_cur):
        for t in range(TILES_PER_ITER):
            j = jj * TILES_PER_ITER + t
            for pair in range(2):
                nxt = (j, 1) if pair == 0 else (jnp.minimum(j + 1, n_qt - 1), 0)
                m_cur, acc = stage(weights(*nxt), 1 - pair, pair, m_cur)
                finish(acc, j, pair)
        return m_cur

    lax.fori_loop(0, n_qt // TILES_PER_ITER, body, m0)


def _attention(qt, k2, vt, kmeta2, vtmeta, w_out, g_mix, w_ffn_in, g_ffn, w_ffn_down,
               n_batch, tq, tk):
    n_rows = k2.shape[0]
    seq = n_rows // n_batch
    n_qt = seq // tq
    gw = KV_GROUP * HEAD_DIM
    n_steps = n_batch * N_KV_HEADS
    kernel = functools.partial(_attn_kernel, tq=tq, tk=tk, seq=seq)

    def slab(w):
        rows = w.shape[0] // n_steps
        assert rows * n_steps == w.shape[0] and rows % 16 == 0, w.shape
        return pl.BlockSpec((rows, w.shape[1]), lambda b, g: (b * N_KV_HEADS + g, 0))

    weights_in = (w_out, g_mix, w_ffn_in, g_ffn, w_ffn_down)
    weights_out = (w_out, w_ffn_in, w_ffn_down)
    return pl.pallas_call(
        kernel,
        grid=(n_batch, N_KV_HEADS),
        in_specs=[
            pl.BlockSpec((seq, gw), lambda b, g: (b, g)),
            pl.BlockSpec((seq, LANES), lambda b, g: (b, g)),
            pl.BlockSpec((1, 1, VT_ROWS, seq), lambda b, g: (b, g, 0, 0)),
            pl.BlockSpec((N_META, LANES), lambda b, g: (0, g)),
            pl.BlockSpec((1, VT_ROWS, LANES), lambda b, g: (g, 0, 0)),
        ] + [slab(w) for w in weights_in],
        out_specs=[pl.BlockSpec((seq, gw), lambda b, g: (b, g))] + [slab(w) for w in weights_out],
        out_shape=[jax.ShapeDtypeStruct((n_rows, D_ATTN), F32)]
        + [jax.ShapeDtypeStruct(w.shape, BF16) for w in weights_out],
        scratch_shapes=[pltpu.VMEM((tk + N_META, LANES), BF16),
                        pltpu.VMEM((VT_ROWS, tk + LANES), BF16),
                        pltpu.VMEM((2, seq + N_META, 2 * tq), F32)],
        compiler_params=pltpu.CompilerParams(
            dimension_semantics=("parallel", "parallel"), vmem_limit_bytes=VMEM_LIMIT),
        name="attention",
    )(qt, k2, vt, kmeta2, vtmeta, *weights_in)


def _mix_ffn_kernel(x_ref, a_ref, u_ref, uprev_ref, unext_ref, umeta_ref,
                    wp_ref, wo_ref, wi_ref, wd_ref, gl_ref,
                    o_ref, ext_ref, act_ref, *, tm, tpb, seq_total):
    j = pl.program_id(0) % tpb

    a = a_ref[...]
    proj_a = jnp.dot(a.astype(BF16), wo_ref[0:D_ATTN, :], preferred_element_type=F32) * _rms(a)

    ext_ref[POOL_HALO:POOL_HALO + tm, :] = u_ref[...]
    ext_ref[0:POOL_HALO, :] = jnp.where(j == 0, umeta_ref[N_META - POOL_HALO:N_META, :],
                                        uprev_ref[...])
    ext_ref[POOL_HALO + tm:2 * POOL_HALO + tm, :] = jnp.where(j == tpb - 1, 0.0, unext_ref[...])
    ext_ref[2 * POOL_HALO + tm:, :] = jnp.zeros((EXT_SLACK, D_POOL), F32)

    t8 = j * tm + (tm - POOL_HALO + N_META) + lax.broadcasted_iota(
        jnp.int32, (POOL_HALO, POOL_GROUP), 0)
    diffs = []
    for gi, w in enumerate(POOL_WINDOWS):
        half = w // 2
        xg = ext_ref[:, gi * POOL_GROUP:(gi + 1) * POOL_GROUP]
        s, k, n = xg, 1, tm + 2 * POOL_HALO + EXT_SLACK
        while k < half:
            n -= POOL_HALO
            s = s[0:n] + s[k:k + n]
            k *= 2
        win = s[POOL_HALO - half:POOL_HALO - half + tm] + s[POOL_HALO:POOL_HALO + tm]
        u_self = xg[POOL_HALO:POOL_HALO + tm]
        cnt8 = jnp.minimum(t8 + half, seq_total) - (t8 - half)
        body = win[:tm - POOL_HALO] * (1.0 / w) - u_self[:tm - POOL_HALO]
        tail = win[tm - POOL_HALO:] * (1.0 / cnt8.astype(F32)) - u_self[tm - POOL_HALO:]
        diffs.append(jnp.concatenate([body, tail], axis=0))
    pooled = []
    for pr in range(2):
        d = jnp.concatenate(diffs[2 * pr:2 * pr + 2], axis=1).astype(BF16)
        pooled.append(jnp.dot(d, wp_ref[pr], preferred_element_type=F32))
    y = jnp.concatenate(pooled, axis=1)
    proj_p = jnp.dot(y.astype(BF16), wo_ref[D_ATTN:, :], preferred_element_type=F32) * _rms(y)
    h1 = x_ref[...] + proj_a + proj_p

    h1b = h1.astype(BF16)
    r1 = _rms(h1)
    for c in range(D_FF // FF_CHUNK):
        cols = slice(c * FF_CHUNK, (c + 1) * FF_CHUNK)
        gate = jnp.dot(h1b, wi_ref[:, cols], preferred_element_type=F32) * r1
        up = jnp.dot(h1b, wi_ref[:, D_FF + c * FF_CHUNK:D_FF + (c + 1) * FF_CHUNK],
                     preferred_element_type=F32) * r1
        act_ref[:, cols] = (gate * jax.nn.sigmoid(gate) * up).astype(BF16)
    h3 = h1 + jnp.dot(act_ref[...], wd_ref[...], preferred_element_type=F32)
    o_ref[...] = h3 * _rms(h3) * gl_ref[...]


def _mix_ffn(x2d, attn, u, umeta, wp, wo, wi, wd, gl, n_batch, tm):
    n_rows = x2d.shape[0]
    tiles = n_rows // tm
    tpb = tiles // n_batch
    hb = tm // POOL_HALO
    n_hblocks = n_rows // POOL_HALO
    kernel = functools.partial(_mix_ffn_kernel, tm=tm, tpb=tpb,
                               seq_total=N_META + n_rows // n_batch)

    def const(shape):
        zeros = (0,) * len(shape)
        return pl.BlockSpec(shape, lambda i: zeros, pipeline_mode=pl.Buffered(1))

    return pl.pallas_call(
        kernel,
        grid=(tiles,),
        in_specs=[
            pl.BlockSpec((tm, D_MODEL), lambda i: (i, 0)),
            pl.BlockSpec((tm, D_ATTN), lambda i: (i, 0)),
            pl.BlockSpec((tm, D_POOL), lambda i: (i, 0)),
            pl.BlockSpec((POOL_HALO, D_POOL), lambda i: (jnp.maximum(i * hb - 1, 0), 0)),
            pl.BlockSpec((POOL_HALO, D_POOL),
                         lambda i: (jnp.minimum((i + 1) * hb, n_hblocks - 1), 0)),
            const((N_META, D_POOL)),
            const((2, MXU_DIM, MXU_DIM)),
            const((D_MODEL, D_MODEL)),
            const((D_MODEL, 2 * D_FF)),
            const((D_FF, D_MODEL)),
            const((1, D_MODEL)),
        ],
        out_specs=pl.BlockSpec((tm, D_MODEL), lambda i: (i, 0)),
        out_shape=jax.ShapeDtypeStruct((n_rows, D_MODEL), F32),
        scratch_shapes=[pltpu.VMEM((tm + 2 * POOL_HALO + EXT_SLACK, D_POOL), F32),
                        pltpu.VMEM((tm, D_FF), BF16)],
        compiler_params=pltpu.CompilerParams(
            dimension_semantics=("parallel",), vmem_limit_bytes=VMEM_LIMIT),
        name="mix_ffn",
    )(x2d, attn, u, u, u, umeta, wp, wo, wi, wd, gl)


def _rope_tables(n_real):
    rows = n_real // GRID_W
    freqs = ROPE_THETA ** (-np.arange(ROPE_AXIS_PAIRS, dtype=np.float64) / ROPE_AXIS_PAIRS)
    ar = (np.arange(rows) - rows // 2).astype(np.float64)[:, None] * freqs
    ac = (np.arange(GRID_W) - GRID_W // 2).astype(np.float64)[:, None] * freqs
    zr = np.zeros((rows, 2 * ROPE_AXIS_PAIRS))
    zc = np.zeros((GRID_W, 2 * ROPE_AXIS_PAIRS))
    tables = (np.concatenate([np.cos(ar), np.cos(ar), zr], axis=1),
              np.concatenate([-np.sin(ar), np.sin(ar), zr], axis=1),
              np.concatenate([zc, np.cos(ac), np.cos(ac)], axis=1),
              np.concatenate([zc, -np.sin(ac), np.sin(ac)], axis=1))
    return tuple(jnp.asarray(np.tile(t, (1, LANES // HEAD_DIM)), dtype=F32) for t in tables)


def kernel(x, meta_tokens, norm_mix, w_in, q_norm, k_norm, attn_out_norm, w_pool,
           pool_scale, w_out, norm_ffn, w_ffn_in, w_ffn_down, norm_final):
    n_batch, n_real, _ = x.shape
    assert n_real % TM_PROJ == 0 and n_real % TQ == 0 and n_real % TM_FFN == 0
    assert w_in.shape[0] == 1, "single layer"
    x2d = x.reshape(n_batch * n_real, D_MODEL)

    w_in_b = (norm_mix[0][:, None] * w_in[0]).astype(BF16)
    head_gain = jnp.concatenate([jnp.tile(q_norm[0], N_HEADS) * (HEAD_DIM ** -0.5 * LOG2_E),
                                 jnp.tile(k_norm[0], N_KV_HEADS)]).reshape(1, D_QK)
    blk = np.arange(MXU_DIM) // HEAD_DIM
    e_heads = jnp.asarray(blk[:, None] == blk[None, :], dtype=BF16)
    rope = _rope_tables(n_real)

    wp = jnp.zeros((2, MXU_DIM, MXU_DIM), F32)
    for gi in range(len(POOL_WINDOWS)):
        r0 = (gi % 2) * POOL_GROUP
        wp = wp.at[gi // 2, r0:r0 + POOL_GROUP, r0:r0 + POOL_GROUP].set(w_pool[0, gi])
    wp = wp.astype(BF16)
    g_mix = jnp.concatenate([attn_out_norm[0], pool_scale[0]]).reshape(D_MODEL, 1)
    g_ffn = norm_ffn[0].reshape(D_MODEL, 1)

    qt, k2, vt, u = _inproj(x2d, w_in_b, head_gain, e_heads, rope, n_batch, TM_PROJ)
    meta_pad = jnp.zeros((TQ, D_MODEL), F32).at[:N_META].set(meta_tokens)
    ones_row = jnp.asarray(np.ones((TQ // GRID_W, LANES), np.float32))
    zero_row = jnp.asarray(np.zeros((TQ // GRID_W, LANES), np.float32))
    zero_col = jnp.asarray(np.zeros((GRID_W, LANES), np.float32))
    _, kmeta2, vtmeta, umeta = _inproj(meta_pad, w_in_b, head_gain, e_heads,
                                       (ones_row, zero_row, zero_col, zero_col), 1, TQ)
    kmeta2 = kmeta2[:N_META]
    lane = jnp.arange(LANES)
    vtmeta = jnp.where(lane[None, None, :] < N_META, vtmeta[0, :, :, :LANES], 0).astype(BF16)
    umeta = umeta[:N_META]

    attn, w_out_b, wi, wd = _attention(qt, k2, vt, kmeta2, vtmeta,
                                       w_out[0], g_mix, w_ffn_in[0], g_ffn, w_ffn_down[0],
                                       n_batch, TQ, TK)

    out = _mix_ffn(x2d, attn, u, umeta, wp, w_out_b, wi, wd,
                   norm_final.reshape(1, D_MODEL), n_batch, TM_FFN)
    return out.reshape(n_batch, n_real, D_MODEL)
```

```python
import functools

import jax
import jax.numpy as jnp
import numpy as np
from jax import lax
from jax.experimental import pallas as pl
from jax.experimental.pallas import tpu as pltpu

D_MODEL = 1024
N_META = 16
GRID_W = 64
N_HEADS = 8
N_KV_HEADS = 2
HEAD_DIM = 64
KV_GROUP = N_HEADS // N_KV_HEADS
D_ATTN = N_HEADS * HEAD_DIM
D_KV = N_KV_HEADS * HEAD_DIM
D_QK = D_ATTN + D_KV
D_POOL = 512
POOL_WINDOWS = (2, 4, 8, 16)
POOL_GROUP = 128
POOL_HALO = 8
EXT_SLACK = 16
D_IN = D_ATTN + 2 * D_KV + D_POOL
D_FF = 2816
ROPE_AXIS_PAIRS = HEAD_DIM // 4
ROPE_THETA = 10000.0
EPS = 1e-6

LANES = 128
MXU_DIM = 256
VT_ROWS = HEAD_DIM + 16
VMEM_LIMIT = 56 * 1024 * 1024

LOG2_E = 1.4426950408889634

TM_PROJ = 1024
X_BUFS = 3
TQ = 256
TK = 256
TILES_PER_ITER = 4
TM_FFN = 512
FF_CHUNK = 256

F32 = jnp.float32
BF16 = jnp.bfloat16


def _rms(x):
    return lax.rsqrt(jnp.mean(x * x, axis=-1, keepdims=True) + EPS)


def _inproj_kernel(x_hbm, w_ref, hg_ref, e_ref, cos_row_ref, sin_row_ref, cos_col_ref, sin_col_ref,
                   q_ref, k_ref, vt_ref, u_ref, xbuf, xsem, *, tm, n_tiles):
    i = pl.program_id(0)

    def x_copy(t, slot):
        rows = pl.ds(pl.multiple_of(t * tm, tm), tm)
        return pltpu.make_async_copy(x_hbm.at[rows, :], xbuf.at[slot], xsem.at[slot])

    @pl.when(i == 0)
    def _():
        for t in range(min(X_BUFS - 1, n_tiles)):
            x_copy(t, t).start()

    @pl.when(i + (X_BUFS - 1) < n_tiles)
    def _():
        x_copy(i + (X_BUFS - 1), (i + (X_BUFS - 1)) % X_BUFS).start()

    slot = i % X_BUFS
    x_copy(i, slot).wait()

    x = xbuf[slot]
    h = x.astype(BF16)
    r = _rms(x)
    e = e_ref[...]

    def table(row_ref, col_ref):
        col = col_ref[...]
        return jnp.concatenate([row_ref[g:g + 1, :] + col for g in range(row_ref.shape[0])],
                               axis=0)

    cos = table(cos_row_ref, cos_col_ref)
    sin = table(sin_row_ref, sin_col_ref)
    lane = lax.broadcasted_iota(jnp.int32, cos.shape, 1)
    first_half = (lane & ROPE_AXIS_PAIRS) == 0

    def project(c0, width):
        return jnp.dot(h, w_ref[:, c0:c0 + width], preferred_element_type=F32) * r

    def head_norm_rope(p, c0):
        width = p.shape[1]
        ss = jnp.dot((p * p).astype(BF16), e[:width, :width], preferred_element_type=F32)
        pn = p * lax.rsqrt(ss * (1.0 / HEAD_DIM) + EPS) * hg_ref[:, c0:c0 + width]
        out = []
        for c in range(width // LANES):
            xc = pn[:, c * LANES:(c + 1) * LANES]
            partner = jnp.where(first_half,
                                pltpu.roll(xc, LANES - ROPE_AXIS_PAIRS, 1),
                                pltpu.roll(xc, ROPE_AXIS_PAIRS, 1))
            out.append(xc * cos + partner * sin)
        return out

    def q_epilogue(p, blk):
        roped = head_norm_rope(p, blk * MXU_DIM)
        for c, rc in enumerate(roped):
            c0 = blk * MXU_DIM + c * LANES
            q_ref[:, c0:c0 + LANES] = rc.astype(BF16)

    def kv_epilogue(kv):
        kc, = head_norm_rope(kv[:, :D_KV], D_ATTN)
        k_ref[:, 0:LANES] = kc.astype(BF16)
        k_ref[:, LANES:2 * LANES] = pltpu.roll(kc, HEAD_DIM, 1).astype(BF16)
        vt = kv[:, D_KV:].T
        ones = jnp.ones((VT_ROWS - HEAD_DIM, vt.shape[1]), BF16)
        for g in range(N_KV_HEADS):
            vt_ref[0, g, 0:HEAD_DIM, :] = vt[g * HEAD_DIM:(g + 1) * HEAD_DIM].astype(BF16)
            vt_ref[0, g, HEAD_DIM:VT_ROWS, :] = ones

    p_q0 = project(0, MXU_DIM)
    p_q1 = project(MXU_DIM, MXU_DIM)
    q_epilogue(p_q0, 0)
    p_kv = project(D_ATTN, 2 * D_KV)
    q_epilogue(p_q1, 1)
    u_ref[:, 0:MXU_DIM] = project(D_QK + D_KV, MXU_DIM)
    kv_epilogue(p_kv)
    u_ref[:, MXU_DIM:] = project(D_QK + D_KV + MXU_DIM, MXU_DIM)


def _inproj(x2d, w, hg, e, rope, n_batch, tm):
    n_rows = x2d.shape[0]
    tiles = n_rows // tm
    tpb = tiles // n_batch
    assert tm % GRID_W == 0
    rpt = tm // GRID_W
    n_pos = rope[0].shape[0] // rpt
    const = lambda i: (0, 0)
    row_spec = pl.BlockSpec((rpt, LANES), lambda i: (i % n_pos, 0))
    col_spec = pl.BlockSpec((GRID_W, LANES), const)
    return pl.pallas_call(
        functools.partial(_inproj_kernel, tm=tm, n_tiles=tiles),
        grid=(tiles,),
        in_specs=[
            pl.BlockSpec(memory_space=pl.ANY),
            pl.BlockSpec((D_MODEL, D_IN), const),
            pl.BlockSpec((1, D_QK), const),
            pl.BlockSpec((MXU_DIM, MXU_DIM), const),
            row_spec, row_spec, col_spec, col_spec,
        ],
        out_specs=[
            pl.BlockSpec((tm, D_ATTN), lambda i: (i, 0)),
            pl.BlockSpec((tm, 2 * D_KV), lambda i: (i, 0)),
            pl.BlockSpec((1, N_KV_HEADS, VT_ROWS, tm), lambda i: (i // tpb, 0, 0, i % tpb)),
            pl.BlockSpec((tm, D_POOL), lambda i: (i, 0)),
        ],
        out_shape=[
            jax.ShapeDtypeStruct((n_rows, D_ATTN), BF16),
            jax.ShapeDtypeStruct((n_rows, 2 * D_KV), BF16),
            jax.ShapeDtypeStruct((n_batch, N_KV_HEADS, VT_ROWS, tpb * tm), BF16),
            jax.ShapeDtypeStruct((n_rows, D_POOL), F32),
        ],
        scratch_shapes=[pltpu.VMEM((X_BUFS, tm, D_MODEL), F32),
                        pltpu.SemaphoreType.DMA((X_BUFS,))],
        compiler_params=pltpu.CompilerParams(
            dimension_semantics=("arbitrary",), vmem_limit_bytes=VMEM_LIMIT),
        name="inproj",
    )(x2d, w, hg, e, *rope)


def _attn_kernel(q_ref, k_ref, vt_ref, kmeta_ref, vtmeta_ref,
                 wo_ref, gmix_ref, wi_ref, gffn_ref, wd_ref,
                 o_ref, wo_out, wi_out, wd_out, kx_ref, vx_ref, s_ref,
                 *, tq, tk, seq):
    wo_out[...] = (wo_ref[...] * gmix_ref[...]).astype(BF16)
    wi_out[...] = (wi_ref[...] * gffn_ref[...]).astype(BF16)
    wd_out[...] = wd_ref[...].astype(BF16)

    n_qt = seq // tq
    nq = 2 * tq
    n_chunks = seq // tk
    n_keys = seq + N_META
    bounds = [(c * tk, (c + 1) * tk) for c in range(n_chunks)]
    bounds[-1] = (bounds[-1][0], n_keys)

    last_lo = bounds[-1][0]
    kx_ref[0:seq - last_lo, :] = k_ref[last_lo:seq, :]
    kx_ref[seq - last_lo:n_keys - last_lo, :] = kmeta_ref[...]
    vx_ref[:, 0:seq - last_lo] = vt_ref[0, 0, :, last_lo:seq]
    vx_ref[:, seq - last_lo:seq - last_lo + LANES] = vtmeta_ref[0, 0]

    def keys(c):
        lo, hi = bounds[c]
        return kx_ref[0:hi - lo, :] if c == n_chunks - 1 else k_ref[lo:hi, :]

    def values_t(c):
        lo, hi = bounds[c]
        return vx_ref[...] if c == n_chunks - 1 else vt_ref[0, 0, :, lo:hi]

    def weights(j, pair):
        rows = pl.ds(pl.multiple_of(j * tq, tq), tq)
        qt = q_ref[rows, pair * LANES:(pair + 1) * LANES].astype(F32).T.astype(BF16)
        z = jnp.zeros((HEAD_DIM, tq), BF16)
        return jnp.concatenate([jnp.concatenate([qt[0:HEAD_DIM], z], axis=0),
                                jnp.concatenate([qt[HEAD_DIM:], z], axis=0)], axis=1)

    def score_chunk(w, slot, c, mx):
        lo, hi = bounds[c]
        s = jnp.dot(keys(c), w, preferred_element_type=F32)
        s_ref[slot, lo:hi, :] = s
        return jnp.maximum(mx, jnp.max(s, axis=0, keepdims=True))

    def pv_chunk(slot, c, m, acc):
        lo, hi = bounds[c]
        p = jnp.exp2(s_ref[slot, lo:hi, :] - m).astype(BF16)
        vt_c = values_t(c)
        if p.shape[0] < vt_c.shape[1]:
            p = jnp.concatenate([p, jnp.zeros((vt_c.shape[1] - p.shape[0], nq), BF16)], axis=0)
        return acc + jnp.dot(vt_c, p, preferred_element_type=F32)

    def stage(w_next, slot_next, slot_cur, m_cur):
        mx = jnp.full((1, nq), -jnp.inf, F32)
        acc = jnp.zeros((VT_ROWS, nq), F32)
        for c in range(n_chunks):
            mx = score_chunk(w_next, slot_next, c, mx)
            acc = pv_chunk(slot_cur, c, m_cur, acc)
        return mx, acc

    def finish(acc, j, pair):
        o = acc[0:HEAD_DIM] * (1.0 / acc[HEAD_DIM:HEAD_DIM + 1])
        both = jnp.concatenate([o[:, :tq], o[:, tq:]], axis=0)
        o_ref[pl.ds(pl.multiple_of(j * tq, tq), tq), pair * LANES:(pair + 1) * LANES] = both.T

    w0 = weights(0, 0)
    m0 = jnp.full((1, nq), -jnp.inf, F32)
    for c in range(n_chunks):
        m0 = score_chunk(w0, 0, c, m0)

    def body(jj, m_cur):
        for t in range(TILES_PER_ITER):
            j = jj * TILES_PER_ITER + t
            for pair in range(2):
                nxt = (j, 1) if pair == 0 else (jnp.minimum(j + 1, n_qt - 1), 0)
                m_cur, acc = stage(weights(*nxt), 1 - pair, pair, m_cur)
                finish(acc, j, pair)
        return m_cur

    lax.fori_loop(0, n_qt // TILES_PER_ITER, body, m0)


def _attention(qt, k2, vt, kmeta2, vtmeta, w_out, g_mix, w_ffn_in, g_ffn, w_ffn_down,
               n_batch, tq, tk):
    n_rows = k2.shape[0]
    seq = n_rows // n_batch
    n_qt = seq // tq
    gw = KV_GROUP * HEAD_DIM
    n_steps = n_batch * N_KV_HEADS
    kernel = functools.partial(_attn_kernel, tq=tq, tk=tk, seq=seq)

    def slab(w):
        rows = w.shape[0] // n_steps
        assert rows * n_steps == w.shape[0] and rows % 16 == 0, w.shape
        return pl.BlockSpec((rows, w.shape[1]), lambda b, g: (b * N_KV_HEADS + g, 0))

    weights_in = (w_out, g_mix, w_ffn_in, g_ffn, w_ffn_down)
    weights_out = (w_out, w_ffn_in, w_ffn_down)
    return pl.pallas_call(
        kernel,
        grid=(n_batch, N_KV_HEADS),
        in_specs=[
            pl.BlockSpec((seq, gw), lambda b, g: (b, g)),
            pl.BlockSpec((seq, LANES), lambda b, g: (b, g)),
            pl.BlockSpec((1, 1, VT_ROWS, seq), lambda b, g: (b, g, 0, 0)),
            pl.BlockSpec((N_META, LANES), lambda b, g: (0, g)),
            pl.BlockSpec((1, 1, VT_ROWS, LANES), lambda b, g: (0, g, 0, 0)),
        ] + [slab(w) for w in weights_in],
        out_specs=[pl.BlockSpec((seq, gw), lambda b, g: (b, g))] + [slab(w) for w in weights_out],
        out_shape=[jax.ShapeDtypeStruct((n_rows, D_ATTN), F32)]
        + [jax.ShapeDtypeStruct(w.shape, BF16) for w in weights_out],
        scratch_shapes=[pltpu.VMEM((tk + N_META, LANES), BF16),
                        pltpu.VMEM((VT_ROWS, tk + LANES), BF16),
                        pltpu.VMEM((2, seq + N_META, 2 * tq), F32)],
        compiler_params=pltpu.CompilerParams(
            dimension_semantics=("parallel", "parallel"), vmem_limit_bytes=VMEM_LIMIT),
        name="attention",
    )(qt, k2, vt, kmeta2, vtmeta, *weights_in)


def _mix_ffn_kernel(x_ref, a_ref, u_ref, uprev_ref, unext_ref, umeta_ref,
                    wp_ref, wo_ref, wi_ref, wd_ref, gl_ref,
                    o_ref, ext_ref, act_ref, *, tm, tpb, seq_total):
    j = pl.program_id(0) % tpb

    a = a_ref[...]
    proj_a = jnp.dot(a.astype(BF16), wo_ref[0:D_ATTN, :], preferred_element_type=F32) * _rms(a)

    ext_ref[POOL_HALO:POOL_HALO + tm, :] = u_ref[...]
    ext_ref[0:POOL_HALO, :] = jnp.where(j == 0, umeta_ref[N_META - POOL_HALO:N_META, :],
                                        uprev_ref[...])
    ext_ref[POOL_HALO + tm:2 * POOL_HALO + tm, :] = jnp.where(j == tpb - 1, 0.0, unext_ref[...])
    ext_ref[2 * POOL_HALO + tm:, :] = jnp.zeros((EXT_SLACK, D_POOL), F32)

    t8 = j * tm + (tm - POOL_HALO + N_META) + lax.broadcasted_iota(
        jnp.int32, (POOL_HALO, POOL_GROUP), 0)
    diffs = []
    for gi, w in enumerate(POOL_WINDOWS):
        half = w // 2
        xg = ext_ref[:, gi * POOL_GROUP:(gi + 1) * POOL_GROUP]
        s, k, n = xg, 1, tm + 2 * POOL_HALO + EXT_SLACK
        while k < half:
            n -= POOL_HALO
            s = s[0:n] + s[k:k + n]
            k *= 2
        win = s[POOL_HALO - half:POOL_HALO - half + tm] + s[POOL_HALO:POOL_HALO + tm]
        u_self = xg[POOL_HALO:POOL_HALO + tm]
        cnt8 = jnp.minimum(t8 + half, seq_total) - (t8 - half)
        body = win[:tm - POOL_HALO] * (1.0 / w) - u_self[:tm - POOL_HALO]
        tail = win[tm - POOL_HALO:] * (1.0 / cnt8.astype(F32)) - u_self[tm - POOL_HALO:]
        diffs.append(jnp.concatenate([body, tail], axis=0))
    pooled = []
    for pr in range(2):
        d = jnp.concatenate(diffs[2 * pr:2 * pr + 2], axis=1).astype(BF16)
        pooled.append(jnp.dot(d, wp_ref[pr], preferred_element_type=F32))
    y = jnp.concatenate(pooled, axis=1)
    proj_p = jnp.dot(y.astype(BF16), wo_ref[D_ATTN:, :], preferred_element_type=F32) * _rms(y)
    h1 = x_ref[...] + proj_a + proj_p

    h1b = h1.astype(BF16)
    r1 = _rms(h1)
    for c in range(D_FF // FF_CHUNK):
        cols = slice(c * FF_CHUNK, (c + 1) * FF_CHUNK)
        gate = jnp.dot(h1b, wi_ref[:, cols], preferred_element_type=F32) * r1
        up = jnp.dot(h1b, wi_ref[:, D_FF + c * FF_CHUNK:D_FF + (c + 1) * FF_CHUNK],
                     preferred_element_type=F32) * r1
        act_ref[:, cols] = (gate * jax.nn.sigmoid(gate) * up).astype(BF16)
    h3 = h1 + jnp.dot(act_ref[...], wd_ref[...], preferred_element_type=F32)
    o_ref[...] = h3 * _rms(h3) * gl_ref[...]


def _mix_ffn(x2d, attn, u, umeta, wp, wo, wi, wd, gl, n_batch, tm):
    n_rows = x2d.shape[0]
    tiles = n_rows // tm
    tpb = tiles // n_batch
    hb = tm // POOL_HALO
    n_hblocks = n_rows // POOL_HALO
    kernel = functools.partial(_mix_ffn_kernel, tm=tm, tpb=tpb,
                               seq_total=N_META + n_rows // n_batch)

    def const(shape):
        zeros = (0,) * len(shape)
        return pl.BlockSpec(shape, lambda i: zeros, pipeline_mode=pl.Buffered(1))

    return pl.pallas_call(
        kernel,
        grid=(tiles,),
        in_specs=[
            pl.BlockSpec((tm, D_MODEL), lambda i: (i, 0)),
            pl.BlockSpec((tm, D_ATTN), lambda i: (i, 0)),
            pl.BlockSpec((tm, D_POOL), lambda i: (i, 0)),
            pl.BlockSpec((POOL_HALO, D_POOL), lambda i: (jnp.maximum(i * hb - 1, 0), 0)),
            pl.BlockSpec((POOL_HALO, D_POOL),
                         lambda i: (jnp.minimum((i + 1) * hb, n_hblocks - 1), 0)),
            const((N_META, D_POOL)),
            const((2, MXU_DIM, MXU_DIM)),
            const((D_MODEL, D_MODEL)),
            const((D_MODEL, 2 * D_FF)),
            const((D_FF, D_MODEL)),
            const((1, D_MODEL)),
        ],
        out_specs=pl.BlockSpec((tm, D_MODEL), lambda i: (i, 0)),
        out_shape=jax.ShapeDtypeStruct((n_rows, D_MODEL), F32),
        scratch_shapes=[pltpu.VMEM((tm + 2 * POOL_HALO + EXT_SLACK, D_POOL), F32),
                        pltpu.VMEM((tm, D_FF), BF16)],
        compiler_params=pltpu.CompilerParams(
            dimension_semantics=("parallel",), vmem_limit_bytes=VMEM_LIMIT),
        name="mix_ffn",
    )(x2d, attn, u, u, u, umeta, wp, wo, wi, wd, gl)


def _rope_tables(n_real):
    rows = n_real // GRID_W
    freqs = ROPE_THETA ** (-np.arange(ROPE_AXIS_PAIRS, dtype=np.float64) / ROPE_AXIS_PAIRS)
    ar = (np.arange(rows) - rows // 2).astype(np.float64)[:, None] * freqs
    ac = (np.arange(GRID_W) - GRID_W // 2).astype(np.float64)[:, None] * freqs
    zr = np.zeros((rows, 2 * ROPE_AXIS_PAIRS))
    zc = np.zeros((GRID_W, 2 * ROPE_AXIS_PAIRS))
    tables = (np.concatenate([np.cos(ar), np.cos(ar), zr], axis=1),
              np.concatenate([-np.sin(ar), np.sin(ar), zr], axis=1),
              np.concatenate([zc, np.cos(ac), np.cos(ac)], axis=1),
              np.concatenate([zc, -np.sin(ac), np.sin(ac)], axis=1))
    return tuple(jnp.asarray(np.tile(t, (1, LANES // HEAD_DIM)), dtype=F32) for t in tables)


def kernel(x, meta_tokens, norm_mix, w_in, q_norm, k_norm, attn_out_norm, w_pool,
           pool_scale, w_out, norm_ffn, w_ffn_in, w_ffn_down, norm_final):
    n_batch, n_real, _ = x.shape
    assert n_real % TM_PROJ == 0 and n_real % TQ == 0 and n_real % TM_FFN == 0
    assert w_in.shape[0] == 1, "single layer"
    x2d = x.reshape(n_batch * n_real, D_MODEL)

    w_in_b = (norm_mix[0][:, None] * w_in[0]).astype(BF16)
    head_gain = jnp.concatenate([jnp.tile(q_norm[0], N_HEADS) * (HEAD_DIM ** -0.5 * LOG2_E),
                                 jnp.tile(k_norm[0], N_KV_HEADS)]).reshape(1, D_QK)
    blk = np.arange(MXU_DIM) // HEAD_DIM
    e_heads = jnp.asarray(blk[:, None] == blk[None, :], dtype=BF16)
    rope = _rope_tables(n_real)

    wp = jnp.zeros((2, MXU_DIM, MXU_DIM), F32)
    for gi in range(len(POOL_WINDOWS)):
        r0 = (gi % 2) * POOL_GROUP
        wp = wp.at[gi // 2, r0:r0 + POOL_GROUP, r0:r0 + POOL_GROUP].set(w_pool[0, gi])
    wp = wp.astype(BF16)
    g_mix = jnp.concatenate([attn_out_norm[0], pool_scale[0]]).reshape(D_MODEL, 1)
    g_ffn = norm_ffn[0].reshape(D_MODEL, 1)

    qt, k2, vt, u = _inproj(x2d, w_in_b, head_gain, e_heads, rope, n_batch, TM_PROJ)
    meta_pad = jnp.zeros((TQ, D_MODEL), F32).at[:N_META].set(meta_tokens)
    ones_row = jnp.asarray(np.ones((TQ // GRID_W, LANES), np.float32))
    zero_row = jnp.asarray(np.zeros((TQ // GRID_W, LANES), np.float32))
    zero_col = jnp.asarray(np.zeros((GRID_W, LANES), np.float32))
    _, kmeta2, vtmeta, umeta = _inproj(meta_pad, w_in_b, head_gain, e_heads,
                                       (ones_row, zero_row, zero_col, zero_col), 1, TQ)

    attn, w_out_b, wi, wd = _attention(qt, k2, vt, kmeta2, vtmeta,
                                       w_out[0], g_mix, w_ffn_in[0], g_ffn, w_ffn_down[0],
                                       n_batch, TQ, TK)

    out = _mix_ffn(x2d, attn, u, umeta, wp, w_out_b, wi, wd,
                   norm_final.reshape(1, D_MODEL), n_batch, TM_FFN)
    return out.reshape(n_batch, n_real, D_MODEL)
```
